```python
import math
import jax
import jax.numpy as jnp
from jax import lax
import numpy as np

D_MODEL = 1024
BATCH = 4
SEQ = 8192
DEPTH = 4

CTX_LEN = 256
GRID_W = 64
N_MOD = 6
EPS = 1e-6

FNET_GROUPS = 4
FNET_GROUP_DIM = 64
FNET_WIDTH = FNET_GROUPS * FNET_GROUP_DIM
S5_GROUP_CH = 16
S5_WIDTH = D_MODEL - FNET_WIDTH
S5_GROUPS = S5_WIDTH // S5_GROUP_CH
S5_STATE = 64
S5_STEP_MIN = 1e-3
S5_STEP_MAX = 1e-1
EVEN_IN = FNET_WIDTH + S5_WIDTH

SGU_GROUPS = 4
SGU_GROUP_DIM = 128
SGU_WIDTH = SGU_GROUPS * SGU_GROUP_DIM
SGU_CHUNK = 128
DIFF_HEADS = 4
DIFF_HEAD_DIM = 64
DIFF_V_DIM = 2 * DIFF_HEAD_DIM
DIFF_QK_WIDTH = DIFF_HEADS * 2 * DIFF_HEAD_DIM
DIFF_WIDTH = DIFF_HEADS * DIFF_V_DIM
ROPE_AXIS_DIM = DIFF_HEAD_DIM // 2
ROPE_BASE = 10000.0
Q_BLOCK = 128
Q_OFF = 2 * SGU_WIDTH
KV_OFF = Q_OFF + DIFF_QK_WIDTH
ODD_IN = KV_OFF + DIFF_QK_WIDTH + DIFF_WIDTH
ODD_OUT = SGU_WIDTH + DIFF_WIDTH

N_EXPERTS = 32
TOP_K = 4
D_EXPERT = D_MODEL
SWIGLU_ALPHA = 1.702
SWIGLU_LIMIT = 7.0
MOE_BLOCK = 256

N_EVEN = (DEPTH + 1) // 2
N_ODD = DEPTH // 2

kernel_name = 'hybrid_fnet_s5_gmlp_diffattn_moe_dit'


def rms_norm(x, g):
    xf = x.astype(jnp.float32)
    y = xf * lax.rsqrt(jnp.mean(xf * xf, axis=-1, keepdims=True) + EPS) * g.astype(jnp.float32)
    return y.astype(x.dtype)


def modulate(h, shift, scale):
    return h * (1.0 + scale) + shift


def fourier_mix(a):
    b_, l_, _ = a.shape
    ag = a.reshape(b_, l_, FNET_GROUPS, FNET_GROUP_DIM).astype(jnp.float32)
    f = jnp.fft.fftn(ag, axes=(1, 3), norm='ortho')
    return f.real.reshape(b_, l_, FNET_WIDTH).astype(a.dtype)


def _linear_combine(e1, e2):
    a1, b1 = e1
    a2, b2 = e2
    return a1 * a2, a2 * b1 + b2


def diag_scan(bu, a_bar, h0, reverse):
    l_ = bu.shape[1]
    if h0 is not None:
        first = l_ - 1 if reverse else 0
        bu = bu.at[:, first].add(a_bar * h0)
    a = jnp.broadcast_to(a_bar, (1, l_) + a_bar.shape)
    _, h = lax.associative_scan(_linear_combine, (a, bu), reverse=reverse, axis=1)
    return h


def s5_mix(u_ctx, u_lat, lam_re, lam_im, log_step, b_re, b_im, c_re, c_im, d_skip, glu_w, glu_b, ctx_out):
    f32 = jnp.float32
    lam = lax.complex(lam_re.astype(f32), lam_im.astype(f32))
    step = jnp.exp(log_step.astype(f32))[..., None]
    a_bar = jnp.exp(lam * step)
    b_bar = ((a_bar - 1.0) / lam)[..., None] * lax.complex(b_re.astype(f32), b_im.astype(f32))
    c_mat = lax.complex(c_re.astype(f32), c_im.astype(f32))
    d = d_skip.astype(f32)

    def groups(u):
        return u.reshape(u.shape[0], u.shape[1], S5_GROUPS, S5_GROUP_CH).astype(f32)

    def drive(ug, dr):
        return jnp.einsum('blgc,gnc->blgn', ug.astype(jnp.complex64), b_bar[dr])

    def readout(h, dr):
        return jnp.einsum('blgn,gcn->blgc', h, c_mat[dr]).real

    def glu_out(y, like):
        y = jax.nn.gelu(y).reshape(like.shape).astype(like.dtype)
        return y * jax.nn.sigmoid(y @ glu_w + glu_b)

    gc = groups(u_ctx)
    gl = groups(u_lat)
    hf_c = diag_scan(drive(gc, 0), a_bar[0], None, False)
    hb_c = diag_scan(drive(gc, 1), a_bar[1], None, True)
    hf_l = diag_scan(drive(gl, 0), a_bar[0], hf_c[:, -1], False)
    hb_l = diag_scan(drive(gl, 1), a_bar[1], hb_c[:, 0], True)
    y_lat = glu_out(readout(hf_l, 0) + readout(hb_l, 1) + d * gl, u_lat)
    y_ctx = glu_out(readout(hf_c, 0) + readout(hb_c, 1) + d * gc, u_ctx) if ctx_out else None
    return y_ctx, y_lat


def even_mixer(n_ctx, n_lat, w_in, w_out, lam_re, lam_im, log_step, b_re, b_im, c_re, c_im, d_skip,
               glu_w, glu_b, ctx_out):
    z_lat = n_lat @ w_in
    if ctx_out:
        z_ctx = n_ctx @ w_in
        u_ctx = z_ctx[..., FNET_WIDTH:]
    else:
        u_ctx = n_ctx @ w_in[:, FNET_WIDTH:]
    y_ctx_s5, y_lat_s5 = s5_mix(u_ctx, z_lat[..., FNET_WIDTH:], lam_re, lam_im, log_step, b_re, b_im,
                                c_re, c_im, d_skip, glu_w, glu_b, ctx_out)
    out_lat = jnp.concatenate([fourier_mix(z_lat[..., :FNET_WIDTH]), y_lat_s5], axis=-1) @ w_out
    out_ctx = None
    if ctx_out:
        out_ctx = jnp.concatenate([fourier_mix(z_ctx[..., :FNET_WIDTH]), y_ctx_s5], axis=-1) @ w_out
    return out_ctx, out_lat


def spatial_gating(u, v, ln_g, ln_b, w_s, b_s):
    b_, l_, _ = u.shape
    nc = l_ // SGU_CHUNK
    vg = v.reshape(b_, nc, SGU_CHUNK, SGU_GROUPS, SGU_GROUP_DIM).astype(jnp.float32)
    mu = jnp.mean(vg, axis=-1, keepdims=True)
    var = jnp.mean(jnp.square(vg - mu), axis=-1, keepdims=True)
    vn = (vg - mu) * lax.rsqrt(var + EPS) * ln_g.astype(jnp.float32) + ln_b.astype(jnp.float32)
    vm = jnp.einsum('bnpgc,gqp->bnqgc', vn, w_s.astype(jnp.float32)) + b_s.astype(jnp.float32).T[:, :, None]
    ug = u.reshape(b_, nc, SGU_CHUNK, SGU_GROUPS, SGU_GROUP_DIM)
    return (ug * vm).reshape(b_, l_, SGU_WIDTH).astype(u.dtype)


def axial_rope_tables(l_):
    rows = l_ // GRID_W
    r, col = jnp.meshgrid(jnp.arange(rows), jnp.arange(GRID_W), indexing='ij')
    inv_freq = jnp.power(ROPE_BASE, -jnp.arange(0, ROPE_AXIS_DIM, 2, dtype=jnp.float32) / ROPE_AXIS_DIM)
    ang_r = r.reshape(-1).astype(jnp.float32)[:, None] * inv_freq
    ang_c = col.reshape(-1).astype(jnp.float32)[:, None] * inv_freq
    ex = lambda a: a[:, None, None, :]
    return ex(jnp.cos(ang_r)), ex(jnp.sin(ang_r)), ex(jnp.cos(ang_c)), ex(jnp.sin(ang_c))


def _rotate(x, cos, sin):
    x1, x2 = jnp.split(x, 2, axis=-1)
    return jnp.concatenate([x1 * cos - x2 * sin, x2 * cos + x1 * sin], axis=-1)


def axial_rope(x, rope):
    cr, sr, cc, sc = rope
    xf = x.astype(jnp.float32)
    y = jnp.concatenate([_rotate(xf[..., :ROPE_AXIS_DIM], cr, sr),
                         _rotate(xf[..., ROPE_AXIS_DIM:], cc, sc)], axis=-1)
    return y.astype(x.dtype)


def diff_attention(q, k, v, lam):
    b_, lq = q.shape[:2]
    nb = lq // Q_BLOCK
    qb = q.reshape(b_, nb, Q_BLOCK, DIFF_HEADS, 2, DIFF_HEAD_DIM).transpose(1, 0, 2, 3, 4, 5)
    scale = DIFF_HEAD_DIM ** -0.5

    def one(qblk):
        s = jnp.einsum('bqhjd,bkhjd->bhjqk', qblk, k, preferred_element_type=jnp.float32) * scale
        p = jax.nn.softmax(s, axis=-1)
        w = (p[:, :, 0] - lam * p[:, :, 1]).astype(v.dtype)
        return jnp.einsum('bhqk,bkhe->bqhe', w, v)

    o = lax.map(one, qb)
    return o.transpose(1, 0, 2, 3, 4).reshape(b_, lq, DIFF_HEADS, DIFF_V_DIM)


def _heads_qk(t):
    return t.reshape(t.shape[0], t.shape[1], DIFF_HEADS, 2, DIFF_HEAD_DIM)


def _heads_v(t):
    return t.reshape(t.shape[0], t.shape[1], DIFF_HEADS, DIFF_V_DIM)


def odd_mixer(n_ctx, n_lat, w_in, w_out, ln_g, ln_b, w_s, b_s, lq1, lk1, lq2, lk2, subln_g, rope,
              lam_init, ctx_out):
    f32 = jnp.float32
    z_lat = n_lat @ w_in
    if ctx_out:
        z_ctx = n_ctx @ w_in
        kv_ctx = z_ctx[..., KV_OFF:]
    else:
        kv_ctx = n_ctx @ w_in[:, KV_OFF:]
    k_ctx = _heads_qk(kv_ctx[..., :DIFF_QK_WIDTH])
    v_ctx = _heads_v(kv_ctx[..., DIFF_QK_WIDTH:])
    lam = (jnp.exp(jnp.sum(lq1.astype(f32) * lk1.astype(f32)))
           - jnp.exp(jnp.sum(lq2.astype(f32) * lk2.astype(f32))) + lam_init)

    def attn_out(q, k, v):
        o = rms_norm(diff_attention(q, k, v, lam), subln_g) * (1.0 - lam_init)
        return o.reshape(q.shape[0], q.shape[1], DIFF_WIDTH)

    def sgu_out(z):
        return spatial_gating(jax.nn.gelu(z[..., :SGU_WIDTH]), jax.nn.gelu(z[..., SGU_WIDTH:Q_OFF]),
                              ln_g, ln_b, w_s, b_s)

    q_lat = axial_rope(_heads_qk(z_lat[..., Q_OFF:KV_OFF]), rope)
    k_lat = axial_rope(_heads_qk(z_lat[..., KV_OFF:KV_OFF + DIFF_QK_WIDTH]), rope)
    v_lat = _heads_v(z_lat[..., KV_OFF + DIFF_QK_WIDTH:])
    k_all = jnp.concatenate([k_ctx, k_lat], axis=1)
    v_all = jnp.concatenate([v_ctx, v_lat], axis=1)
    out_lat = jnp.concatenate([sgu_out(z_lat), attn_out(q_lat, k_all, v_all)], axis=-1) @ w_out
    out_ctx = None
    if ctx_out:
        q_ctx = _heads_qk(z_ctx[..., Q_OFF:KV_OFF])
        out_ctx = jnp.concatenate([sgu_out(z_ctx), attn_out(q_ctx, k_ctx, v_ctx)], axis=-1) @ w_out
    return out_ctx, out_lat


def moe_ffn(h, router_w, router_b, w1, b1, w2, b2):
    n_tok, d_ = h.shape
    logits = (h @ router_w).astype(jnp.float32) + router_b.astype(jnp.float32)
    top_val, top_idx = lax.top_k(logits, TOP_K)
    gates = jax.nn.softmax(top_val, axis=-1).astype(h.dtype)
    n_assign = n_tok * TOP_K
    flat_e = top_idx.reshape(-1)
    order = jnp.argsort(flat_e)
    e_sorted = flat_e[order]
    tok_sorted = order // TOP_K
    gate_sorted = gates.reshape(-1)[order]
    counts = jnp.bincount(flat_e, length=N_EXPERTS)
    starts = jnp.cumsum(counts) - counts
    padded = (counts + MOE_BLOCK - 1) // MOE_BLOCK * MOE_BLOCK
    pad_ends = jnp.cumsum(padded)
    pad_starts = pad_ends - padded
    dest = pad_starts[e_sorted] + jnp.arange(n_assign) - starts[e_sorted]
    n_rows = (-(-n_assign // MOE_BLOCK) + N_EXPERTS) * MOE_BLOCK
    n_blocks = n_rows // MOE_BLOCK
    block_e = jnp.minimum(jnp.searchsorted(pad_ends, jnp.arange(n_blocks) * MOE_BLOCK, side='right'),
                          N_EXPERTS - 1)
    x_rows = jnp.zeros((n_rows, d_), h.dtype).at[dest].set(h[tok_sorted])

    def expert_block(args):
        xb, e = args
        hh = xb @ w1[e] + b1[e]
        glu = jnp.minimum(hh[:, :D_EXPERT], SWIGLU_LIMIT)
        lin = jnp.clip(hh[:, D_EXPERT:], -SWIGLU_LIMIT, SWIGLU_LIMIT)
        act = glu * jax.nn.sigmoid(SWIGLU_ALPHA * glu) * (lin + 1.0)
        return act @ w2[e] + b2[e]

    y_rows = lax.map(expert_block, (x_rows.reshape(n_blocks, MOE_BLOCK, d_), block_e)).reshape(n_rows, d_)
    y = y_rows[dest] * gate_sorted[:, None]
    return jax.ops.segment_sum(y, tok_sorted, num_segments=n_tok)


def setup_inputs(seed: int = 0) -> dict:
    key = jax.random.key(seed)
    ks = iter(jax.random.split(key, 48))
    f32 = jnp.float32

    def nrm(shape, scale):
        return scale * jax.random.normal(next(ks), shape, f32)

    n_idx = jnp.arange(S5_STATE, dtype=f32)
    s5_shape = (N_EVEN, 2, S5_GROUPS, S5_STATE)
    return {
        'x': nrm((BATCH, SEQ, D_MODEL), 1.0),
        'c': nrm((BATCH, D_MODEL), 1.0),
        'ctx': nrm((BATCH, CTX_LEN, D_MODEL), 1.0),
        'c_ctx': nrm((D_MODEL,), 1.0),
        'ada_w': nrm((DEPTH, D_MODEL, N_MOD * D_MODEL), 0.5 * D_MODEL ** -0.5),
        'ada_b': nrm((DEPTH, N_MOD * D_MODEL), 0.01),
        'norm1_g': 1.0 + nrm((DEPTH, D_MODEL), 0.02),
        'norm2_g': 1.0 + nrm((DEPTH, D_MODEL), 0.02),
        'ev_w_in': nrm((N_EVEN, D_MODEL, EVEN_IN), D_MODEL ** -0.5),
        'ev_w_out': nrm((N_EVEN, EVEN_IN, D_MODEL), EVEN_IN ** -0.5),
        's5_lam_re': -0.5 + nrm(s5_shape, 0.01),
        's5_lam_im': math.pi * n_idx + nrm(s5_shape, 0.01),
        's5_log_step': jax.random.uniform(next(ks), (N_EVEN, 2, S5_GROUPS), f32,
                                          math.log(S5_STEP_MIN), math.log(S5_STEP_MAX)),
        's5_b_re': nrm((N_EVEN, 2, S5_GROUPS, S5_STATE, S5_GROUP_CH), (2 * S5_GROUP_CH) ** -0.5),
        's5_b_im': nrm((N_EVEN, 2, S5_GROUPS, S5_STATE, S5_GROUP_CH), (2 * S5_GROUP_CH) ** -0.5),
        's5_c_re': nrm((N_EVEN, 2, S5_GROUPS, S5_GROUP_CH, S5_STATE), (2 * S5_STATE) ** -0.5),
        's5_c_im': nrm((N_EVEN, 2, S5_GROUPS, S5_GROUP_CH, S5_STATE), (2 * S5_STATE) ** -0.5),
        's5_d': nrm((N_EVEN, S5_GROUPS, S5_GROUP_CH), 1.0),
        's5_glu_w': nrm((N_EVEN, S5_WIDTH, S5_WIDTH), S5_WIDTH ** -0.5),
        's5_glu_b': nrm((N_EVEN, S5_WIDTH), 0.01),
        'od_w_in': nrm((N_ODD, D_MODEL, ODD_IN), D_MODEL ** -0.5),
        'od_w_out': nrm((N_ODD, ODD_OUT, D_MODEL), ODD_OUT ** -0.5),
        'sgu_ln_g': 1.0 + nrm((N_ODD, SGU_GROUPS, SGU_GROUP_DIM), 0.02),
        'sgu_ln_b': nrm((N_ODD, SGU_GROUPS, SGU_GROUP_DIM), 0.02),
        'sgu_w': nrm((N_ODD, SGU_GROUPS, SGU_CHUNK, SGU_CHUNK), SGU_CHUNK ** -0.5),
        'sgu_b': 1.0 + nrm((N_ODD, SGU_GROUPS, SGU_CHUNK), 0.1),
        'diff_lq1': nrm((N_ODD, DIFF_HEAD_DIM), 0.1),
        'diff_lk1': nrm((N_ODD, DIFF_HEAD_DIM), 0.1),
        'diff_lq2': nrm((N_ODD, DIFF_HEAD_DIM), 0.1),
        'diff_lk2': nrm((N_ODD, DIFF_HEAD_DIM), 0.1),
        'diff_subln_g': 1.0 + nrm((N_ODD, DIFF_V_DIM), 0.02),
        'router_w': nrm((DEPTH, D_MODEL, N_EXPERTS), D_MODEL ** -0.5),
        'router_b': nrm((DEPTH, N_EXPERTS), 0.01),
        'exp_w1': nrm((DEPTH, N_EXPERTS, D_MODEL, 2 * D_EXPERT), D_MODEL ** -0.5),
        'exp_b1': nrm((DEPTH, N_EXPERTS, 2 * D_EXPERT), 0.01),
        'exp_w2': nrm((DEPTH, N_EXPERTS, D_EXPERT, D_MODEL), D_EXPERT ** -0.5),
        'exp_b2': nrm((DEPTH, N_EXPERTS, D_MODEL), 0.01),
        'final_g': 1.0 + nrm((D_MODEL,), 0.02),
    }


def reference(x, c, ctx, c_ctx, ada_w, ada_b, norm1_g, norm2_g, ev_w_in, ev_w_out, s5_lam_re, s5_lam_im,
              s5_log_step, s5_b_re, s5_b_im, s5_c_re, s5_c_im, s5_d, s5_glu_w, s5_glu_b, od_w_in, od_w_out,
              sgu_ln_g, sgu_ln_b, sgu_w, sgu_b, diff_lq1, diff_lk1, diff_lq2, diff_lk2, diff_subln_g,
              router_w, router_b, exp_w1, exp_b1, exp_w2, exp_b2, final_g):
    b_, l_, d_ = x.shape
    rope = axial_rope_tables(l_)
    c_act = jax.nn.silu(c)
    cc_act = jax.nn.silu(c_ctx)
    h_lat, h_ctx = x, ctx
    for i in range(DEPTH):
        ctx_out = i < DEPTH - 1
        sh1, sc1, g1, sh2, sc2, g2 = jnp.split((c_act @ ada_w[i] + ada_b[i])[:, None, :], N_MOD, axis=-1)
        csh1, csc1, cg1, csh2, csc2, cg2 = jnp.split(cc_act @ ada_w[i] + ada_b[i], N_MOD, axis=-1)
        n_lat = modulate(rms_norm(h_lat, norm1_g[i]), sh1, sc1)
        n_ctx = modulate(rms_norm(h_ctx, norm1_g[i]), csh1, csc1)
        j = i // 2
        if i % 2 == 0:
            y_ctx, y_lat = even_mixer(n_ctx, n_lat, ev_w_in[j], ev_w_out[j], s5_lam_re[j], s5_lam_im[j],
                                      s5_log_step[j], s5_b_re[j], s5_b_im[j], s5_c_re[j], s5_c_im[j],
                                      s5_d[j], s5_glu_w[j], s5_glu_b[j], ctx_out)
        else:
            lam_init = 0.8 - 0.6 * math.exp(-0.3 * i)
            y_ctx, y_lat = odd_mixer(n_ctx, n_lat, od_w_in[j], od_w_out[j], sgu_ln_g[j], sgu_ln_b[j], sgu_w[j],
                                     sgu_b[j], diff_lq1[j], diff_lk1[j], diff_lq2[j], diff_lk2[j],
                                     diff_subln_g[j], rope, lam_init, ctx_out)
        h_lat = h_lat + g1 * y_lat
        m_lat = modulate(rms_norm(h_lat, norm2_g[i]), sh2, sc2)
        f_lat = moe_ffn(m_lat.reshape(b_ * l_, d_), router_w[i], router_b[i], exp_w1[i], exp_b1[i],
                        exp_w2[i], exp_b2[i])
        h_lat = h_lat + g2 * f_lat.reshape(b_, l_, d_)
        if ctx_out:
            h_ctx = h_ctx + cg1 * y_ctx
            m_ctx = modulate(rms_norm(h_ctx, norm2_g[i]), csh2, csc2)
            f_ctx = moe_ffn(m_ctx.reshape(-1, d_), router_w[i], router_b[i], exp_w1[i], exp_b1[i],
                            exp_w2[i], exp_b2[i])
            h_ctx = h_ctx + cg2 * f_ctx.reshape(h_ctx.shape)
    return rms_norm(h_lat, final_g)
```

```python
import functools
import math

import numpy as np
import jax
import jax.numpy as jnp
from jax import lax
from jax.experimental import pallas as pl
from jax.experimental.pallas import tpu as pltpu

F32 = jnp.float32
BF16 = jnp.bfloat16
I32 = jnp.int32

D = 1024
N_MOD = 6
EPS = 1e-6
LANES = 128
SUBLANES = 8
ROW_TILES = D // LANES

FNET_W = 256
FNET_GD = 64
S5_W = 768
S5_C = 16
S5_G = S5_W // S5_C
S5_N = 64
S5_T = 64

SGU_W = 512
SGU_CHUNK = 128
SGU_GROUPS = 4
HEADS = 4
HEAD_DIM = 64
V_DIM = 128
QK_W = 512
Q_OFF = 2 * SGU_W
KV_OFF = Q_OFF + QK_W
ODD_IN = KV_OFF + QK_W + HEADS * V_DIM
GRID_W = 64
ROPE_AXIS_DIM = HEAD_DIM // 2
ROPE_BASE = 10000.0

N_EXPERTS = 32
TOP_K = 4
D_EXPERT = 1024
SWIGLU_ALPHA = 1.702
SWIGLU_LIMIT = 7.0
MOE_BLOCK = 256

TM = 256
TQ = 256
TK = 512
NEG = -3.0e38
VMEM_LIMIT = 56 * 1024 * 1024


def _cparams(*sem):
    return pltpu.CompilerParams(dimension_semantics=sem, vmem_limit_bytes=VMEM_LIMIT)


def _rms(x, g):
    return x * lax.rsqrt(jnp.mean(x * x, axis=-1, keepdims=True) + EPS) * g


def _adaln_kernel(c_ref, w_ref, b_ref, o_ref):
    c = c_ref[...]
    a = (c * jax.nn.sigmoid(c)).astype(BF16)
    o_ref[0] = jnp.dot(a, w_ref[0].astype(BF16), preferred_element_type=F32) + b_ref[0]


def _adaln(cvec, ada_w, ada_b):
    depth, _, n = ada_w.shape
    tn = 1536
    nb = cvec.shape[0]
    return pl.pallas_call(
        _adaln_kernel,
        grid=(depth, n // tn),
        in_specs=[pl.BlockSpec((nb, D), lambda i, j: (0, 0)),
                  pl.BlockSpec((1, D, tn), lambda i, j: (i, 0, j)),
                  pl.BlockSpec((1, 1, tn), lambda i, j: (i, 0, j))],
        out_specs=pl.BlockSpec((1, nb, tn), lambda i, j: (i, 0, j)),
        out_shape=jax.ShapeDtypeStruct((depth, nb, n), F32),
        compiler_params=_cparams("parallel", "parallel"),
        name="adaln",
    )(cvec, ada_w, ada_b.reshape(depth, 1, n))


def _mod_spec(geom):
    n_lat_tiles, tiles_per_batch, nb = geom
    return pl.BlockSpec((None, 1, D),
                        lambda i: (jnp.where(i < n_lat_tiles, i // tiles_per_batch, nb), 0, 0))


def _norm_in_kernel(h_ref, g_ref, sh_ref, sc_ref, w_ref, o_ref):
    n = _rms(h_ref[...], g_ref[...]) * (1.0 + sc_ref[...]) + sh_ref[...]
    o_ref[...] = jnp.dot(n.astype(BF16), w_ref[...], preferred_element_type=F32)


def _norm_in(h, g, shift, scale, w_bf16, geom):
    nt = h.shape[0]
    n = w_bf16.shape[1]
    return pl.pallas_call(
        _norm_in_kernel,
        grid=(nt // TM,),
        in_specs=[pl.BlockSpec((TM, D), lambda i: (i, 0)),
                  pl.BlockSpec((1, D), lambda i: (0, 0)),
                  _mod_spec(geom), _mod_spec(geom),
                  pl.BlockSpec((D, n), lambda i: (0, 0))],
        out_specs=pl.BlockSpec((TM, n), lambda i: (i, 0)),
        out_shape=jax.ShapeDtypeStruct((nt, n), F32),
        compiler_params=_cparams("parallel"),
        name="norm_in_proj",
    )(h, g.reshape(1, D), shift, scale, w_bf16)


def _out_res_kernel(a1_ref, a2_ref, w1_ref, w2_ref, h_ref, gate_ref, o_ref):
    y = jnp.dot(a1_ref[...].astype(BF16), w1_ref[...], preferred_element_type=F32)
    y = y + jnp.dot(a2_ref[...].astype(BF16), w2_ref[...], preferred_element_type=F32)
    o_ref[...] = h_ref[...] + gate_ref[...] * y


def _out_res(a1, a2, w_bf16, h, gate, geom):
    nt = h.shape[0]
    k1, k2 = a1.shape[1], a2.shape[1]
    return pl.pallas_call(
        _out_res_kernel,
        grid=(nt // TM,),
        in_specs=[pl.BlockSpec((TM, k1), lambda i: (i, 0)),
                  pl.BlockSpec((TM, k2), lambda i: (i, 0)),
                  pl.BlockSpec((k1, D), lambda i: (0, 0)),
                  pl.BlockSpec((k2, D), lambda i: (0, 0)),
                  pl.BlockSpec((TM, D), lambda i: (i, 0)),
                  _mod_spec(geom)],
        out_specs=pl.BlockSpec((TM, D), lambda i: (i, 0)),
        out_shape=jax.ShapeDtypeStruct((nt, D), F32),
        compiler_params=_cparams("parallel"),
        name="out_proj_residual",
    )(a1, a2, w_bf16[:k1], w_bf16[k1:], h, gate)


def _dft_cs(n, scale):
    k = np.arange(n)
    ang = 2.0 * np.pi * ((k[:, None] * k[None, :]) % n) / n
    return np.cos(ang) * scale, np.sin(ang) * scale


def _blockdiag(m, reps):
    n = m.shape[0]
    out = np.zeros((n * reps, n * reps), m.dtype)
    for r in range(reps):
        out[r * n:(r + 1) * n, r * n:(r + 1) * n] = m
    return out


def _fnet_lat_kernel(x_ref, cc_ref, sc_ref, gr_ref, gi_ref, c2_ref, s2_ref, o_ref, yr_ref, yi_ref, *, n1):
    x = x_ref[...].astype(BF16)
    yr_ref[...] = jnp.dot(x, cc_ref[...], preferred_element_type=F32)
    yi_ref[...] = jnp.dot(x, sc_ref[...], preferred_element_type=F32)

    def stage1(t2, carry):
        rows = pl.ds(t2, n1, stride=LANES)
        xr = yr_ref[rows, :].astype(BF16)
        xi = yi_ref[rows, :].astype(BF16)
        gr = gr_ref[t2]
        gi = gi_ref[t2]
        yr_ref[rows, :] = (jnp.dot(gr, xr, preferred_element_type=F32)
                           - jnp.dot(gi, xi, preferred_element_type=F32))
        yi_ref[rows, :] = (jnp.dot(gr, xi, preferred_element_type=F32)
                           + jnp.dot(gi, xr, preferred_element_type=F32))
        return carry

    lax.fori_loop(0, LANES, stage1, 0)

    def stage2(k1, carry):
        rows = pl.ds(pl.multiple_of(k1 * LANES, LANES), LANES)
        yr = yr_ref[rows, :].astype(BF16)
        yi = yi_ref[rows, :].astype(BF16)
        out = (jnp.dot(c2_ref[...], yr, preferred_element_type=F32)
               + jnp.dot(s2_ref[...], yi, preferred_element_type=F32))
        o_ref[pl.ds(k1, LANES, stride=n1), :] = out
        return carry

    lax.fori_loop(0, n1, stage2, 0)


def _fnet_lat(z, nb, l):
    nt = z.shape[0]
    n1 = l // LANES
    cg, sg = _dft_cs(FNET_GD, FNET_GD ** -0.5)
    cc = _blockdiag(cg, LANES // FNET_GD)
    sc = -_blockdiag(sg, LANES // FNET_GD)
    t2 = np.arange(LANES)[:, None, None]
    k1 = np.arange(n1)[None, :, None]
    t1 = np.arange(n1)[None, None, :]
    ang = 2.0 * np.pi * ((k1 * (t2 + LANES * t1)) % l) / l
    gr, gi = np.cos(ang), -np.sin(ang)
    c2, s2 = _dft_cs(LANES, l ** -0.5)
    bf = lambda a: jnp.asarray(a, F32).astype(BF16)
    const2 = lambda b, hf: (0, 0)
    const3 = lambda b, hf: (0, 0, 0)
    return pl.pallas_call(
        functools.partial(_fnet_lat_kernel, n1=n1),
        grid=(nb, FNET_W // LANES),
        in_specs=[pl.BlockSpec((l, LANES), lambda b, hf: (b, hf)),
                  pl.BlockSpec((LANES, LANES), const2), pl.BlockSpec((LANES, LANES), const2),
                  pl.BlockSpec((LANES, n1, n1), const3), pl.BlockSpec((LANES, n1, n1), const3),
                  pl.BlockSpec((LANES, LANES), const2), pl.BlockSpec((LANES, LANES), const2)],
        out_specs=pl.BlockSpec((l, LANES), lambda b, hf: (b, hf)),
        out_shape=jax.ShapeDtypeStruct((nt, FNET_W), F32),
        scratch_shapes=[pltpu.VMEM((l, LANES), F32), pltpu.VMEM((l, LANES), F32)],
        compiler_params=_cparams("parallel", "parallel"),
        name="fnet_latent",
    )(z, bf(cc), bf(sc), bf(gr), bf(gi), bf(c2), bf(s2))


def _fnet_ctx_kernel(x_ref, cc_ref, sc_ref, cl_ref, sl_ref, prev_ref, o_ref):
    del prev_ref
    x = x_ref[...].astype(BF16)
    xc = jnp.dot(x, cc_ref[...], preferred_element_type=F32).astype(BF16)
    xs = jnp.dot(x, sc_ref[...], preferred_element_type=F32).astype(BF16)
    o_ref[...] = (jnp.dot(cl_ref[...], xc, preferred_element_type=F32)
                  - jnp.dot(sl_ref[...], xs, preferred_element_type=F32))


def _fnet_ctx(z, fnet_out, nb, l, ctx):
    cg, sg = _dft_cs(FNET_GD, FNET_GD ** -0.5)
    cc = _blockdiag(cg, FNET_W // FNET_GD)
    sc = _blockdiag(sg, FNET_W // FNET_GD)
    cl, sl = _dft_cs(ctx, ctx ** -0.5)
    bf = lambda a: jnp.asarray(a, F32).astype(BF16)
    off = nb * l // ctx
    const = lambda b: (0, 0)
    return pl.pallas_call(
        _fnet_ctx_kernel,
        grid=(nb,),
        in_specs=[pl.BlockSpec((ctx, FNET_W), lambda b: (off + b, 0)),
                  pl.BlockSpec((FNET_W, FNET_W), const), pl.BlockSpec((FNET_W, FNET_W), const),
                  pl.BlockSpec((ctx, ctx), const), pl.BlockSpec((ctx, ctx), const),
                  pl.BlockSpec(memory_space=pl.ANY)],
        out_specs=pl.BlockSpec((ctx, FNET_W), lambda b: (off + b, 0)),
        out_shape=jax.ShapeDtypeStruct(fnet_out.shape, F32),
        input_output_aliases={5: 0},
        compiler_params=_cparams("parallel"),
        name="fnet_context",
    )(z, bf(cc), bf(sc), bf(cl), bf(sl), fnet_out)


def _s5_tables(lam_re, lam_im, log_step, b_re, b_im, c_re, c_im, d_skip):
    t = S5_T
    hp = lax.Precision.HIGHEST
    step = jnp.exp(log_step.astype(F32))[..., None]
    lr = lam_re.astype(F32) * step
    li = lam_im.astype(F32) * step
    k = jnp.arange(t + 1, dtype=F32)
    mag = jnp.exp(lr[..., None] * k)
    ang = li[..., None] * k
    pr, pi = mag * jnp.cos(ang), mag * jnp.sin(ang)
    ar, ai = pr[..., 1], pi[..., 1]
    den = lam_re.astype(F32) ** 2 + lam_im.astype(F32) ** 2
    qr = ((ar - 1.0) * lam_re + ai * lam_im) / den
    qi = (ai * lam_re - (ar - 1.0) * lam_im) / den
    br = qr[..., None] * b_re - qi[..., None] * b_im
    bi = qr[..., None] * b_im + qi[..., None] * b_re
    cr, ci = c_re.astype(F32), c_im.astype(F32)
    wr = pr[..., None] * br[..., None, :] - pi[..., None] * bi[..., None, :]
    wi = pr[..., None] * bi[..., None, :] + pi[..., None] * br[..., None, :]
    kern = (jnp.einsum('xgcn,xgnkd->xgkcd', cr, wr, precision=hp)
            - jnp.einsum('xgcn,xgnkd->xgkcd', ci, wi, precision=hp))
    tt = np.arange(t)
    lag_f = np.clip(tt[:, None] - tt[None, :], 0, None)
    lag_b = np.clip(tt[None, :] - tt[:, None], 0, None)
    kf = kern[0][:, lag_f]
    kb = kern[1][:, lag_b]
    diag = kern[0][:, 0] + kern[1][:, 0] + jax.vmap(jnp.diag)(d_skip.astype(F32))
    tri = (tt[:, None] - tt[None, :])[None, :, :, None, None]
    m = jnp.where(tri > 0, kf, jnp.where(tri < 0, kb, diag[:, None, None]))
    g = m.shape[0]
    mt = m.transpose(0, 2, 4, 1, 3).reshape(g, t * S5_C, t * S5_C)
    pf_r = wr[0][:, :, t - 1 - tt].transpose(0, 2, 3, 1).reshape(g, t * S5_C, S5_N)
    pf_i = wi[0][:, :, t - 1 - tt].transpose(0, 2, 3, 1).reshape(g, t * S5_C, S5_N)
    pb_r = wr[1][:, :, tt].transpose(0, 2, 3, 1).reshape(g, t * S5_C, S5_N)
    pb_i = wi[1][:, :, tt].transpose(0, 2, 3, 1).reshape(g, t * S5_C, S5_N)
    p = jnp.concatenate([pf_r, pf_i, pb_r, pb_i], axis=-1)
    def readout(x, idx):
        zr = cr[x][:, :, :, None] * pr[x][:, None, :, idx] - ci[x][:, :, :, None] * pi[x][:, None, :, idx]
        zi = cr[x][:, :, :, None] * pi[x][:, None, :, idx] + ci[x][:, :, :, None] * pr[x][:, None, :, idx]
        to_rows = lambda a: a.transpose(0, 2, 3, 1).reshape(g, S5_N, t * S5_C)
        return to_rows(zr), to_rows(-zi)
    qf_r, qf_i = readout(0, tt + 1)
    qb_r, qb_i = readout(1, t - tt)
    q = jnp.concatenate([qf_r, qf_i, qb_r, qb_i], axis=1)
    at_r, at_i = pr[..., t], pi[..., t]
    return mt.astype(BF16), p.astype(BF16), q.astype(BF16), at_r, at_i


def _s5_local_kernel(u_ref, p_ref, o_ref):
    o_ref[0] = jnp.dot(u_ref[0].astype(BF16), p_ref[0], preferred_element_type=F32)


def _s5_local(u, p):
    g, r, w = u.shape
    n4 = p.shape[2]
    return pl.pallas_call(
        _s5_local_kernel,
        grid=(g,),
        in_specs=[pl.BlockSpec((1, r, w), lambda i: (i, 0, 0)),
                  pl.BlockSpec((1, w, n4), lambda i: (i, 0, 0))],
        out_specs=pl.BlockSpec((1, r, n4), lambda i: (i, 0, 0)),
        out_shape=jax.ShapeDtypeStruct((g, r, n4), F32),
        compiler_params=_cparams("parallel"),
        name="s5_chunk_states",
    )(u, p)


def _s5_scan_kernel(sr_ref, si_ref, ar_ref, ai_ref, or_ref, oi_ref):
    ar, ai = ar_ref[...], ai_ref[...]

    def body(s, carry):
        hr, hi = carry
        or_ref[s] = hr
        oi_ref[s] = hi
        return (ar * hr - ai * hi + sr_ref[s], ar * hi + ai * hr + si_ref[s])

    zero = jnp.zeros(ar.shape, F32)
    lax.fori_loop(0, sr_ref.shape[0], body, (zero, zero))


def _s5_scan(sr, si, ar, ai):
    nch, rows, w = sr.shape
    wb = 512
    blk = pl.BlockSpec((nch, rows, wb), lambda i: (0, 0, i))
    cst = pl.BlockSpec((rows, wb), lambda i: (0, i))
    return pl.pallas_call(
        _s5_scan_kernel,
        grid=(w // wb,),
        in_specs=[blk, blk, cst, cst],
        out_specs=[blk, blk],
        out_shape=[jax.ShapeDtypeStruct(sr.shape, F32)] * 2,
        compiler_params=_cparams("parallel"),
        name="s5_chunk_scan",
    )(sr, si, ar, ai)


def _s5_apply_kernel(u_ref, mt_ref, h_ref, q_ref, o_ref):
    y = jnp.dot(u_ref[0].astype(BF16), mt_ref[0], preferred_element_type=F32)
    o_ref[0] = y + jnp.dot(h_ref[0].astype(BF16), q_ref[0], preferred_element_type=F32)


def _s5_apply(u, mt, hin, q):
    g, r, w = u.shape
    n4 = hin.shape[2]
    return pl.pallas_call(
        _s5_apply_kernel,
        grid=(g,),
        in_specs=[pl.BlockSpec((1, r, w), lambda i: (i, 0, 0)),
                  pl.BlockSpec((1, w, w), lambda i: (i, 0, 0)),
                  pl.BlockSpec((1, r, n4), lambda i: (i, 0, 0)),
                  pl.BlockSpec((1, n4, w), lambda i: (i, 0, 0))],
        out_specs=pl.BlockSpec((1, r, w), lambda i: (i, 0, 0)),
        out_shape=jax.ShapeDtypeStruct((g, r, w), F32),
        compiler_params=_cparams("parallel"),
        name="s5_apply",
    )(u, mt, hin, q)


def _glu_kernel(y_ref, w_ref, b_ref, o_ref):
    y = jax.nn.gelu(y_ref[...])
    gate = jnp.dot(y.astype(BF16), w_ref[...], preferred_element_type=F32) + b_ref[...]
    o_ref[...] = y * jax.nn.sigmoid(gate)


def _glu(y, w_bf16, b):
    nt, w = y.shape
    return pl.pallas_call(
        _glu_kernel,
        grid=(nt // TM,),
        in_specs=[pl.BlockSpec((TM, w), lambda i: (i, 0)),
                  pl.BlockSpec((w, w), lambda i: (0, 0)),
                  pl.BlockSpec((1, w), lambda i: (0, 0))],
        out_specs=pl.BlockSpec((TM, w), lambda i: (i, 0)),
        out_shape=jax.ShapeDtypeStruct((nt, w), F32),
        compiler_params=_cparams("parallel"),
        name="s5_glu",
    )(y, w_bf16, b.reshape(1, w))


def _s5_mix(z, nb, l, ctx, tables, glu_w, glu_b):
    mt, p, q, at_r, at_i = tables
    t = S5_T
    ncl, ncc = l // t, ctx // t
    nch = ncl + ncc
    u_lat = z[:nb * l, FNET_W:].reshape(nb, ncl, t, S5_G, S5_C)
    u_ctx = z[nb * l:, FNET_W:].reshape(nb, ncc, t, S5_G, S5_C)
    u = jnp.concatenate([u_ctx, u_lat], axis=1)
    u = u.transpose(3, 1, 0, 2, 4).reshape(S5_G, nch * nb, t * S5_C)
    s = _s5_local(u, p).reshape(S5_G, nch, nb, 4, S5_N)
    order_b = np.concatenate([np.arange(ncc - 1, -1, -1), np.arange(nch - 1, ncc - 1, -1)])
    inv_b = np.argsort(order_b)
    to_scan = lambda a: a.transpose(1, 2, 0, 3).reshape(nch, nb, S5_G * S5_N)
    pad = (-2 * nb) % SUBLANES
    def scan_in(j):
        a = jnp.concatenate([to_scan(s[:, :, :, j]), to_scan(s[:, order_b][:, :, :, 2 + j])], axis=1)
        return jnp.pad(a, ((0, 0), (0, pad), (0, 0)))
    def scan_const(a):
        rows = jnp.concatenate([jnp.broadcast_to(a[0].reshape(1, -1), (nb, S5_G * S5_N)),
                                jnp.broadcast_to(a[1].reshape(1, -1), (nb, S5_G * S5_N))], axis=0)
        return jnp.pad(rows, ((0, pad), (0, 0)))
    hr, hi = _s5_scan(scan_in(0), scan_in(1), scan_const(at_r), scan_const(at_i))
    from_scan = lambda a: a.reshape(nch, nb, S5_G, S5_N).transpose(2, 0, 1, 3)
    hin = jnp.stack([from_scan(hr[:, :nb]), from_scan(hi[:, :nb]),
                     from_scan(hr[:, nb:2 * nb])[:, inv_b], from_scan(hi[:, nb:2 * nb])[:, inv_b]], axis=3)
    hin = hin.reshape(S5_G, nch * nb, 4 * S5_N)
    y = _s5_apply(u, mt, hin, q).reshape(S5_G, nch, nb, t, S5_C).transpose(2, 1, 3, 0, 4)
    y_ctx = y[:, :ncc].reshape(nb * ctx, S5_W)
    y_lat = y[:, ncc:].reshape(nb * l, S5_W)
    return _glu(jnp.concatenate([y_lat, y_ctx], axis=0), glu_w.astype(BF16), glu_b)


def _sgu_kernel(z_ref, lng_ref, lnb_ref, ws_ref, bs_ref, o_ref):
    for ch in range(TM // SGU_CHUNK):
        rows = slice(ch * SGU_CHUNK, (ch + 1) * SGU_CHUNK)
        for g in range(SGU_GROUPS):
            cols = slice(g * LANES, (g + 1) * LANES)
            u = jax.nn.gelu(z_ref[rows, g * LANES:(g + 1) * LANES])
            v = jax.nn.gelu(z_ref[rows, SGU_W + g * LANES:SGU_W + (g + 1) * LANES])
            mu = jnp.mean(v, axis=-1, keepdims=True)
            vc = v - mu
            var = jnp.mean(vc * vc, axis=-1, keepdims=True)
            vn = vc * lax.rsqrt(var + EPS) * lng_ref[g:g + 1, :] + lnb_ref[g:g + 1, :]
            vm = jnp.dot(ws_ref[g], vn.astype(BF16), preferred_element_type=F32) + bs_ref[:, g:g + 1]
            o_ref[rows, cols] = u * vm


def _sgu(z, ln_g, ln_b, w_s, b_s):
    nt = z.shape[0]
    return pl.pallas_call(
        _sgu_kernel,
        grid=(nt // TM,),
        in_specs=[pl.BlockSpec((TM, 2 * SGU_W), lambda i: (i, 0)),
                  pl.BlockSpec((SGU_GROUPS, LANES), lambda i: (0, 0)),
                  pl.BlockSpec((SGU_GROUPS, LANES), lambda i: (0, 0)),
                  pl.BlockSpec((SGU_GROUPS, SGU_CHUNK, SGU_CHUNK), lambda i: (0, 0, 0)),
                  pl.BlockSpec((SGU_CHUNK, SGU_GROUPS), lambda i: (0, 0))],
        out_specs=pl.BlockSpec((TM, SGU_W), lambda i: (i, 0)),
        out_shape=jax.ShapeDtypeStruct((nt, SGU_W), F32),
        compiler_params=_cparams("parallel"),
        name="spatial_gating",
    )(z, ln_g, ln_b, w_s.astype(BF16), b_s.T)


def _rope_tables(l):
    t = np.arange(l)
    inv_freq = ROPE_BASE ** (-np.arange(0, ROPE_AXIS_DIM, 2, dtype=np.float32) / ROPE_AXIS_DIM)
    inv_freq = inv_freq.astype(np.float32)
    d = np.arange(LANES) % HEAD_DIM
    pos = np.where(d[None, :] < ROPE_AXIS_DIM, (t // GRID_W)[:, None], (t % GRID_W)[:, None]).astype(np.float32)
    ang = pos * inv_freq[d % (ROPE_AXIS_DIM // 2)][None, :]
    upper = (d % ROPE_AXIS_DIM) >= ROPE_AXIS_DIM // 2
    cos = np.cos(ang)
    sin_lo = np.where(upper[None, :], np.sin(ang), 0.0)
    sin_hi = np.where(upper[None, :], 0.0, -np.sin(ang))
    ident = lambda a, v: np.concatenate([a, np.full((TM, LANES), v)], axis=0).astype(np.float32)
    return jnp.asarray(ident(cos, 1.0)), jnp.asarray(ident(sin_lo, 0.0)), jnp.asarray(ident(sin_hi, 0.0))


def _qkv_kernel(q_ref, k_ref, v_ref, cos_ref, slo_ref, shi_ref, qo_ref, ko_ref, vo_ref):
    cos, slo, shi = cos_ref[...], slo_ref[...], shi_ref[...]
    half = ROPE_AXIS_DIM // 2
    for h in range(HEADS):
        cols = slice(h * LANES, (h + 1) * LANES)
        for src, dst, scale in ((q_ref, qo_ref, HEAD_DIM ** -0.5), (k_ref, ko_ref, 1.0)):
            x = src[:, cols]
            y = x * cos + pltpu.roll(x, half, 1) * slo + pltpu.roll(x, LANES - half, 1) * shi
            dst[:, cols] = (y * scale).astype(BF16)
    vo_ref[...] = v_ref[...].astype(BF16)


def _qkv(z, rope, geom):
    nt = z.shape[0]
    n_lat_tiles, tiles_per_batch, _ = geom
    tab = pl.BlockSpec((TM, LANES), lambda i: (jnp.where(i < n_lat_tiles, i % tiles_per_batch, tiles_per_batch), 0))
    col = lambda c: pl.BlockSpec((TM, QK_W), lambda i: (i, c))
    out = pl.BlockSpec((TM, QK_W), lambda i: (i, 0))
    return pl.pallas_call(
        _qkv_kernel,
        grid=(nt // TM,),
        in_specs=[col(Q_OFF // QK_W), col(KV_OFF // QK_W), col((KV_OFF + QK_W) // QK_W), tab, tab, tab],
        out_specs=[out, out, out],
        out_shape=[jax.ShapeDtypeStruct((nt, QK_W), BF16)] * 3,
        compiler_params=_cparams("parallel"),
        name="qkv_rope",
    )(z, z, z, *rope)


def _attn_kernel(lam_ref, q_ref, kc_ref, vc_ref, *rest, n_lat_chunks, out_scale):
    if n_lat_chunks:
        kl_ref, vl_ref, g_ref, o_ref = rest
    else:
        g_ref, _, o_ref = rest
    q = q_ref[...]
    lane = lax.broadcasted_iota(I32, q.shape, 1)
    zero = jnp.zeros_like(q)
    qs = (jnp.where(lane < HEAD_DIM, q, zero), jnp.where(lane >= HEAD_DIM, q, zero))
    nt_dims = (((1,), (1,)), ((), ()))

    def step(k, v, carry):
        out = []
        for j in range(2):
            m, l, a = carry[j]
            s = lax.dot_general(qs[j], k, nt_dims, preferred_element_type=F32)
            mn = jnp.maximum(m, jnp.max(s, axis=-1, keepdims=True))
            alpha = jnp.exp(m - mn)
            p = jnp.exp(s - mn)
            l = alpha * l + jnp.sum(p, axis=-1, keepdims=True)
            a = alpha * a + jnp.dot(p.astype(BF16), v, preferred_element_type=F32)
            out.append((mn, l, a))
        return tuple(out)

    rows = q.shape[0]
    init = tuple((jnp.full((rows, 1), NEG, F32), jnp.zeros((rows, 1), F32), jnp.zeros((rows, V_DIM), F32))
                 for _ in range(2))
    carry = step(kc_ref[...], vc_ref[...], init)
    if n_lat_chunks:
        def body(c, carry):
            ks = pl.ds(pl.multiple_of(c * TK, TK), TK)
            return step(kl_ref[ks, :], vl_ref[ks, :], carry)
        carry = lax.fori_loop(0, n_lat_chunks, body, carry)
    (_, l1, a1), (_, l2, a2) = carry
    o = a1 / l1 - lam_ref[0] * (a2 / l2)
    o_ref[...] = _rms(o, g_ref[...]) * out_scale


def _attention(q, k, v, lam, subln_g, nb, l, ctx, lam_init):
    nt = q.shape[0]
    g = subln_g.reshape(1, V_DIM)
    scale = 1.0 - lam_init
    smem = pl.BlockSpec(memory_space=pltpu.SMEM)
    off = nb * l // ctx
    ctx_blk = lambda f: pl.BlockSpec((ctx, LANES), f)
    gspec3 = pl.BlockSpec((1, V_DIM), lambda b, h, i: (0, 0))
    any_spec = pl.BlockSpec(memory_space=pl.ANY)
    nq = l // TQ
    lat = pl.pallas_call(
        functools.partial(_attn_kernel, n_lat_chunks=l // TK, out_scale=scale),
        grid=(nb, HEADS, nq),
        in_specs=[smem,
                  pl.BlockSpec((TQ, LANES), lambda b, h, i: (b * nq + i, h)),
                  ctx_blk(lambda b, h, i: (off + b, h)), ctx_blk(lambda b, h, i: (off + b, h)),
                  pl.BlockSpec((l, LANES), lambda b, h, i: (b, h)),
                  pl.BlockSpec((l, LANES), lambda b, h, i: (b, h)),
                  gspec3],
        out_specs=pl.BlockSpec((TQ, LANES), lambda b, h, i: (b * nq + i, h)),
        out_shape=jax.ShapeDtypeStruct((nt, HEADS * V_DIM), F32),
        compiler_params=_cparams("parallel", "parallel", "arbitrary"),
        name="diff_attention_latent",
    )
    ctx_call = pl.pallas_call(
        functools.partial(_attn_kernel, n_lat_chunks=0, out_scale=scale),
        grid=(nb, HEADS),
        in_specs=[smem,
                  ctx_blk(lambda b, h: (off + b, h)),
                  ctx_blk(lambda b, h: (off + b, h)), ctx_blk(lambda b, h: (off + b, h)),
                  pl.BlockSpec((1, V_DIM), lambda b, h: (0, 0)), any_spec],
        out_specs=ctx_blk(lambda b, h: (off + b, h)),
        out_shape=jax.ShapeDtypeStruct((nt, HEADS * V_DIM), F32),
        input_output_aliases={5: 0},
        compiler_params=_cparams("parallel", "parallel"),
        name="diff_attention_context",
    )
    return ctx_call(lam, q, k, v, g, lat(lam, q, k, v, k, v, g))


def _router_kernel(h_ref, g_ref, sh_ref, sc_ref, rw_ref, rb_ref, tri_ref, mrt_ref, meta_ref, cnt_ref, carry_ref):
    @pl.when(pl.program_id(0) == 0)
    def _():
        carry_ref[...] = jnp.zeros_like(carry_ref)

    m = _rms(h_ref[...], g_ref[...]) * (1.0 + sc_ref[...]) + sh_ref[...]
    for j in range(ROW_TILES):
        mrt_ref[pl.ds(j, TM, stride=ROW_TILES), :] = m[:, j * LANES:(j + 1) * LANES]
    logits = jnp.dot(m, rw_ref[...], preferred_element_type=F32, precision=lax.Precision.HIGHEST) + rb_ref[...]
    lane = lax.broadcasted_iota(I32, logits.shape, 1).astype(F32)
    multihot = jnp.zeros(logits.shape, F32)
    vals, idxs, sels = [], [], []
    for _ in range(TOP_K):
        mx = jnp.max(logits, axis=-1, keepdims=True)
        idx = jnp.min(jnp.where(logits == mx, lane, float(LANES)), axis=-1, keepdims=True)
        sel = lane == idx
        logits = jnp.where(sel, NEG, logits)
        multihot = jnp.where(sel, 1.0, multihot)
        vals.append(mx)
        idxs.append(idx)
        sels.append(sel)
    ex = [jnp.exp(v - vals[0]) for v in vals]
    den = ex[0] + ex[1] + ex[2] + ex[3]
    cum = jnp.dot(tri_ref[...], multihot.astype(BF16), preferred_element_type=F32) + carry_ref[0:1, :]
    meta = jnp.zeros(logits.shape, F32)
    for k in range(TOP_K):
        rank = jnp.sum(jnp.where(sels[k], cum, 0.0), axis=-1, keepdims=True)
        meta = jnp.where(lane == k, idxs[k], meta)
        meta = jnp.where(lane == TOP_K + k, rank, meta)
        meta = jnp.where(lane == 2 * TOP_K + k, ex[k] / den, meta)
    meta_ref[...] = meta
    carry_ref[...] = carry_ref[...] + jnp.sum(multihot, axis=0, keepdims=True)
    cnt_ref[...] = carry_ref[...]


def _router(h, g, shift, scale, router_w, router_b, geom):
    nt = h.shape[0]
    rw = jnp.pad(router_w.astype(F32), ((0, 0), (0, LANES - N_EXPERTS)))
    rb = jnp.pad(router_b.astype(F32), (0, LANES - N_EXPERTS), constant_values=-1.0e30).reshape(1, LANES)
    tri = jnp.asarray(np.tril(np.ones((TM, TM), np.float32), -1)).astype(BF16)
    return pl.pallas_call(
        _router_kernel,
        grid=(nt // TM,),
        in_specs=[pl.BlockSpec((TM, D), lambda i: (i, 0)),
                  pl.BlockSpec((1, D), lambda i: (0, 0)),
                  _mod_spec(geom), _mod_spec(geom),
                  pl.BlockSpec((D, LANES), lambda i: (0, 0)),
                  pl.BlockSpec((1, LANES), lambda i: (0, 0)),
                  pl.BlockSpec((TM, TM), lambda i: (0, 0))],
        out_specs=[pl.BlockSpec((TM * ROW_TILES, LANES), lambda i: (i, 0)),
                   pl.BlockSpec((TM, LANES), lambda i: (i, 0)),
                   pl.BlockSpec((SUBLANES, LANES), lambda i: (0, 0))],
        out_shape=[jax.ShapeDtypeStruct((nt * ROW_TILES, LANES), F32),
                   jax.ShapeDtypeStruct((nt, LANES), F32),
                   jax.ShapeDtypeStruct((SUBLANES, LANES), F32)],
        scratch_shapes=[pltpu.VMEM((SUBLANES, LANES), F32)],
        compiler_params=_cparams("arbitrary"),
        name="moe_router",
    )(h, g.reshape(1, D), shift, scale, rw, rb, tri)


def _dispatch_kernel(dest_ref, m_ref, zero_ref, x_ref, sem):
    del zero_ref
    i = pl.program_id(0)
    n_copies = TM * TOP_K

    def row_copy(src_row, dst_row):
        return pltpu.make_async_copy(
            m_ref.at[pl.ds(pl.multiple_of(src_row * ROW_TILES, ROW_TILES), ROW_TILES), :],
            x_ref.at[pl.ds(pl.multiple_of(dst_row * ROW_TILES, ROW_TILES), ROW_TILES), :], sem)

    def body(r, carry):
        for k in range(TOP_K):
            row_copy(i * TM + r, dest_ref[0, r * TOP_K + k]).start()
        return carry

    lax.fori_loop(0, TM, body, 0)
    pltpu.make_async_copy(m_ref.at[pl.ds(0, n_copies * ROW_TILES), :],
                          x_ref.at[pl.ds(0, n_copies * ROW_TILES), :], sem).wait()


def _dispatch(dest, m_rt, n_rows):
    nt = m_rt.shape[0] // ROW_TILES
    n_tiles = nt // TM
    return pl.pallas_call(
        _dispatch_kernel,
        grid=(n_tiles,),
        in_specs=[pl.BlockSpec((None, 1, TM * TOP_K), lambda i: (i, 0, 0), memory_space=pltpu.SMEM),
                  pl.BlockSpec(memory_space=pl.ANY),
                  pl.BlockSpec(memory_space=pl.ANY)],
        out_specs=pl.BlockSpec(memory_space=pl.ANY),
        out_shape=jax.ShapeDtypeStruct((n_rows * ROW_TILES, LANES), F32),
        scratch_shapes=[pltpu.SemaphoreType.DMA(())],
        input_output_aliases={2: 0},
        compiler_params=_cparams("arbitrary"),
        name="moe_dispatch",
    )(dest.reshape(n_tiles, 1, TM * TOP_K), m_rt, jnp.zeros((n_rows * ROW_TILES, LANES), F32))


def _expert_kernel(be_ref, nu_ref, x_ref, w1_ref, b1_ref, w2_ref, b2_ref, o_ref, w1b_ref, w2b_ref):
    b = pl.program_id(0)
    prev = be_ref[jnp.maximum(b - 1, 0)]
    fresh = jnp.logical_or(b == 0, be_ref[b] != prev)

    @pl.when(jnp.logical_and(fresh, b < nu_ref[0]))
    def _():
        w1b_ref[...] = w1_ref[0].astype(BF16)
        w2b_ref[...] = w2_ref[0].astype(BF16)

    @pl.when(b < nu_ref[0])
    def _():
        x = jnp.concatenate([x_ref[pl.ds(j, MOE_BLOCK, stride=ROW_TILES), :] for j in range(ROW_TILES)], axis=1)
        hh = jnp.dot(x.astype(BF16), w1b_ref[...], preferred_element_type=F32) + b1_ref[0]
        glu = jnp.minimum(hh[:, :D_EXPERT], SWIGLU_LIMIT)
        lin = jnp.clip(hh[:, D_EXPERT:], -SWIGLU_LIMIT, SWIGLU_LIMIT)
        act = glu * jax.nn.sigmoid(SWIGLU_ALPHA * glu) * (lin + 1.0)
        y = jnp.dot(act.astype(BF16), w2b_ref[...], preferred_element_type=F32) + b2_ref[0]
        for j in range(ROW_TILES):
            o_ref[pl.ds(j, MOE_BLOCK, stride=ROW_TILES), :] = y[:, j * LANES:(j + 1) * LANES]


def _experts(block_e, n_used, x_rt, w1, b1, w2, b2):
    n_blocks = block_e.shape[0]
    ne = w1.shape[0]
    blk = lambda b, be, nu: (jnp.minimum(b, nu[0] - 1), 0)
    exp3 = lambda b, be, nu: (be[jnp.minimum(b, nu[0] - 1)], 0, 0)
    grid_spec = pltpu.PrefetchScalarGridSpec(
        num_scalar_prefetch=2,
        grid=(n_blocks,),
        in_specs=[pl.BlockSpec((MOE_BLOCK * ROW_TILES, LANES), blk),
                  pl.BlockSpec((1, D, 2 * D_EXPERT), exp3),
                  pl.BlockSpec((1, 1, 2 * D_EXPERT), exp3),
                  pl.BlockSpec((1, D_EXPERT, D), exp3),
                  pl.BlockSpec((1, 1, D), exp3)],
        out_specs=pl.BlockSpec((MOE_BLOCK * ROW_TILES, LANES), blk),
        scratch_shapes=[pltpu.VMEM((D, 2 * D_EXPERT), BF16), pltpu.VMEM((D_EXPERT, D), BF16)],
    )
    return pl.pallas_call(
        _expert_kernel,
        grid_spec=grid_spec,
        out_shape=jax.ShapeDtypeStruct(x_rt.shape, F32),
        compiler_params=_cparams("arbitrary"),
        name="moe_experts",
    )(block_e, n_used, x_rt, w1, b1.reshape(ne, 1, -1), w2, b2.reshape(ne, 1, -1))


def _combine_kernel(dest_ref, y_ref, h_ref, meta_ref, gate_ref, fg_ref, o_ref, buf_ref, sem, *, final):
    def row_copy(k, r, src_row):
        return pltpu.make_async_copy(
            y_ref.at[pl.ds(pl.multiple_of(src_row * ROW_TILES, ROW_TILES), ROW_TILES), :],
            buf_ref.at[k, pl.ds(pl.multiple_of(r * ROW_TILES, ROW_TILES), ROW_TILES), :], sem.at[k])

    def body(r, carry):
        for k in range(TOP_K):
            row_copy(k, r, dest_ref[0, r * TOP_K + k]).start()
        return carry

    lax.fori_loop(0, TM, body, 0)
    for k in range(TOP_K):
        pltpu.make_async_copy(y_ref.at[pl.ds(0, TM * ROW_TILES), :], buf_ref.at[k], sem.at[k]).wait()
    meta = meta_ref[...]
    gates = [meta[:, 2 * TOP_K + k:2 * TOP_K + k + 1] for k in range(TOP_K)]
    outs = []
    for j in range(ROW_TILES):
        cols = slice(j * LANES, (j + 1) * LANES)
        f = gates[0] * buf_ref[0, pl.ds(j, TM, stride=ROW_TILES), :]
        for k in range(1, TOP_K):
            f = f + gates[k] * buf_ref[k, pl.ds(j, TM, stride=ROW_TILES), :]
        outs.append(h_ref[:, cols] + gate_ref[:, cols] * f)
    if final:
        hn = jnp.concatenate(outs, axis=1)
        o_ref[...] = _rms(hn, fg_ref[...])
    else:
        for j in range(ROW_TILES):
            o_ref[:, j * LANES:(j + 1) * LANES] = outs[j]


def _combine(dest, y_rt, h, meta, gate, final_g, geom, n_tiles, final):
    return pl.pallas_call(
        functools.partial(_combine_kernel, final=final),
        grid=(n_tiles,),
        in_specs=[pl.BlockSpec((None, 1, TM * TOP_K), lambda i: (i, 0, 0), memory_space=pltpu.SMEM),
                  pl.BlockSpec(memory_space=pl.ANY),
                  pl.BlockSpec((TM, D), lambda i: (i, 0)),
                  pl.BlockSpec((TM, LANES), lambda i: (i, 0)),
                  _mod_spec(geom),
                  pl.BlockSpec((1, D), lambda i: (0, 0))],
        out_specs=pl.BlockSpec((TM, D), lambda i: (i, 0)),
        out_shape=jax.ShapeDtypeStruct((n_tiles * TM, D), F32),
        scratch_shapes=[pltpu.VMEM((TOP_K, TM * ROW_TILES, LANES), F32), pltpu.SemaphoreType.DMA((TOP_K,))],
        compiler_params=_cparams("arbitrary"),
        name="moe_combine",
    )(dest.reshape(-1, 1, TM * TOP_K), y_rt, h, meta, gate, final_g.reshape(1, D))


def _moe(h, g, shift, scale, gate, router_w, router_b, w1, b1, w2, b2, final_g, geom, n_out_tiles, final):
    nt = h.shape[0]
    m_rt, meta, cnt = _router(h, g, shift, scale, router_w, router_b, geom)
    expert = meta[:, :TOP_K].astype(I32)
    rank = meta[:, TOP_K:2 * TOP_K].astype(I32)
    counts = cnt[0, :N_EXPERTS].astype(I32)
    padded = (counts + MOE_BLOCK - 1) // MOE_BLOCK * MOE_BLOCK
    pad_ends = jnp.cumsum(padded)
    pad_starts = pad_ends - padded
    dest = (pad_starts[expert] + rank).reshape(-1)
    n_blocks = nt * TOP_K // MOE_BLOCK + N_EXPERTS
    block_e = jnp.minimum(jnp.searchsorted(pad_ends, jnp.arange(n_blocks) * MOE_BLOCK, side='right'),
                          N_EXPERTS - 1).astype(I32)
    n_used = (pad_ends[-1:] // MOE_BLOCK).astype(I32)
    x_rt = _dispatch(dest, m_rt, n_blocks * MOE_BLOCK)
    y_rt = _experts(block_e, n_used, x_rt, w1, b1, w2, b2)
    return _combine(dest, y_rt, h, meta, gate, final_g, geom, n_out_tiles, final)


def kernel(x, c, ctx, c_ctx, ada_w, ada_b, norm1_g, norm2_g, ev_w_in, ev_w_out, s5_lam_re, s5_lam_im,
           s5_log_step, s5_b_re, s5_b_im, s5_c_re, s5_c_im, s5_d, s5_glu_w, s5_glu_b, od_w_in, od_w_out,
           sgu_ln_g, sgu_ln_b, sgu_w, sgu_b, diff_lq1, diff_lk1, diff_lq2, diff_lk2, diff_subln_g,
           router_w, router_b, exp_w1, exp_b1, exp_w2, exp_b2, final_g):
    nb, l, d = x.shape
    n_ctx = ctx.shape[1]
    depth = ada_w.shape[0]
    assert d == D and l % TM == 0 and n_ctx % TM == 0 and l % TK == 0
    assert TM % n_ctx == 0 or n_ctx % TM == 0
    n_lat = nb * l
    geom = (n_lat // TM, l // TM, nb)

    nbp = -(-(nb + 1) // SUBLANES) * SUBLANES
    cvec = jnp.zeros((nbp, D), F32).at[:nb].set(c).at[nb].set(c_ctx)
    mods = _adaln(cvec, ada_w, ada_b)
    mods = mods.reshape(depth, nbp, N_MOD, 1, D).transpose(0, 2, 1, 3, 4)

    h = jnp.concatenate([x.reshape(n_lat, D), ctx.reshape(nb * n_ctx, D)], axis=0)
    rope = _rope_tables(l)

    for i in range(depth):
        sh1, sc1, g1, sh2, sc2, g2 = (mods[i, j] for j in range(N_MOD))
        last = i == depth - 1
        j = i // 2
        if i % 2 == 0:
            z = _norm_in(h, norm1_g[i], sh1, sc1, ev_w_in[j].astype(BF16), geom)
            fnet = _fnet_ctx(z, _fnet_lat(z, nb, l), nb, l, n_ctx)
            tables = _s5_tables(s5_lam_re[j], s5_lam_im[j], s5_log_step[j], s5_b_re[j], s5_b_im[j],
                                s5_c_re[j], s5_c_im[j], s5_d[j])
            s5 = _s5_mix(z, nb, l, n_ctx, tables, s5_glu_w[j], s5_glu_b[j])
            h = _out_res(fnet, s5, ev_w_out[j].astype(BF16), h, g1, geom)
        else:
            lam_init = 0.8 - 0.6 * math.exp(-0.3 * i)
            z = _norm_in(h, norm1_g[i], sh1, sc1, od_w_in[j].astype(BF16), geom)
            sgu = _sgu(z, sgu_ln_g[j], sgu_ln_b[j], sgu_w[j], sgu_b[j])
            q, k, v = _qkv(z, rope, geom)
            lam = (jnp.exp(jnp.sum(diff_lq1[j].astype(F32) * diff_lk1[j].astype(F32)))
                   - jnp.exp(jnp.sum(diff_lq2[j].astype(F32) * diff_lk2[j].astype(F32))) + lam_init)
            attn = _attention(q, k, v, lam.reshape(1), diff_subln_g[j], nb, l, n_ctx, lam_init)
            h = _out_res(sgu, attn, od_w_out[j].astype(BF16), h, g1, geom)
        n_out_tiles = n_lat // TM if last else h.shape[0] // TM
        h = _moe(h, norm2_g[i], sh2, sc2, g2, router_w[i], router_b[i], exp_w1[i], exp_b1[i],
                 exp_w2[i], exp_b2[i], final_g, geom, n_out_tiles, last)
    return h.reshape(nb, l, D)
```

```python
import functools
import math

import numpy as np
import jax
import jax.numpy as jnp
from jax import lax
from jax.experimental import pallas as pl
from jax.experimental.pallas import tpu as pltpu

F32 = jnp.float32
BF16 = jnp.bfloat16
I32 = jnp.int32

D = 1024
N_MOD = 6
EPS = 1e-6
LANES = 128
SUBLANES = 8
ROW_TILES = D // LANES

FNET_W = 256
FNET_GD = 64
S5_W = 768
S5_C = 16
S5_G = S5_W // S5_C
S5_N = 64
S5_T = 64

SGU_W = 512
SGU_CHUNK = 128
SGU_GROUPS = 4
HEADS = 4
HEAD_DIM = 64
V_DIM = 128
QK_W = 512
Q_OFF = 2 * SGU_W
KV_OFF = Q_OFF + QK_W
ODD_IN = KV_OFF + QK_W + HEADS * V_DIM
GRID_W = 64
ROPE_AXIS_DIM = HEAD_DIM // 2
ROPE_BASE = 10000.0

N_EXPERTS = 32
TOP_K = 4
D_EXPERT = 1024
SWIGLU_ALPHA = 1.702
SWIGLU_LIMIT = 7.0
MOE_BLOCK = 256

TM = 256
TQ = 512
TK = 512
NEG = -3.0e38
VMEM_LIMIT = 56 * 1024 * 1024


def _cparams(*sem):
    return pltpu.CompilerParams(dimension_semantics=sem, vmem_limit_bytes=VMEM_LIMIT)


def _rms(x, g):
    return x * lax.rsqrt(jnp.mean(x * x, axis=-1, keepdims=True) + EPS) * g


def _adaln_kernel(c_ref, w_ref, b_ref, o_ref):
    c = c_ref[...]
    a = (c * jax.nn.sigmoid(c)).astype(BF16)
    o_ref[0] = jnp.dot(a, w_ref[0].astype(BF16), preferred_element_type=F32) + b_ref[0]


def _adaln(cvec, ada_w, ada_b):
    depth, _, n = ada_w.shape
    tn = 1536
    nb = cvec.shape[0]
    return pl.pallas_call(
        _adaln_kernel,
        grid=(depth, n // tn),
        in_specs=[pl.BlockSpec((nb, D), lambda i, j: (0, 0)),
                  pl.BlockSpec((1, D, tn), lambda i, j: (i, 0, j)),
                  pl.BlockSpec((1, 1, tn), lambda i, j: (i, 0, j))],
        out_specs=pl.BlockSpec((1, nb, tn), lambda i, j: (i, 0, j)),
        out_shape=jax.ShapeDtypeStruct((depth, nb, n), F32),
        compiler_params=_cparams("parallel", "parallel"),
        name="adaln",
    )(cvec, ada_w, ada_b.reshape(depth, 1, n))


def _mod_spec(geom):
    n_lat_tiles, tiles_per_batch, nb = geom
    return pl.BlockSpec((None, 1, D),
                        lambda i: (jnp.where(i < n_lat_tiles, i // tiles_per_batch, nb), 0, 0))


def _norm_in_kernel(h_ref, g_ref, sh_ref, sc_ref, w_ref, o_ref):
    n = _rms(h_ref[...], g_ref[...]) * (1.0 + sc_ref[...]) + sh_ref[...]
    o_ref[...] = jnp.dot(n.astype(BF16), w_ref[...], preferred_element_type=F32)


def _norm_in(h, g, shift, scale, w_bf16, geom):
    nt = h.shape[0]
    n = w_bf16.shape[1]
    return pl.pallas_call(
        _norm_in_kernel,
        grid=(nt // TM,),
        in_specs=[pl.BlockSpec((TM, D), lambda i: (i, 0)),
                  pl.BlockSpec((1, D), lambda i: (0, 0)),
                  _mod_spec(geom), _mod_spec(geom),
                  pl.BlockSpec((D, n), lambda i: (0, 0))],
        out_specs=pl.BlockSpec((TM, n), lambda i: (i, 0)),
        out_shape=jax.ShapeDtypeStruct((nt, n), F32),
        compiler_params=_cparams("parallel"),
        name="norm_in_proj",
    )(h, g.reshape(1, D), shift, scale, w_bf16)


def _out_res_kernel(a1_ref, a2_ref, w1_ref, w2_ref, h_ref, gate_ref, o_ref):
    y = jnp.dot(a1_ref[...].astype(BF16), w1_ref[...], preferred_element_type=F32)
    y = y + jnp.dot(a2_ref[...].astype(BF16), w2_ref[...], preferred_element_type=F32)
    o_ref[...] = h_ref[...] + gate_ref[...] * y


def _out_res(a1, a2, w_bf16, h, gate, geom):
    nt = h.shape[0]
    k1, k2 = a1.shape[1], a2.shape[1]
    return pl.pallas_call(
        _out_res_kernel,
        grid=(nt // TM,),
        in_specs=[pl.BlockSpec((TM, k1), lambda i: (i, 0)),
                  pl.BlockSpec((TM, k2), lambda i: (i, 0)),
                  pl.BlockSpec((k1, D), lambda i: (0, 0)),
                  pl.BlockSpec((k2, D), lambda i: (0, 0)),
                  pl.BlockSpec((TM, D), lambda i: (i, 0)),
                  _mod_spec(geom)],
        out_specs=pl.BlockSpec((TM, D), lambda i: (i, 0)),
        out_shape=jax.ShapeDtypeStruct((nt, D), F32),
        compiler_params=_cparams("parallel"),
        name="out_proj_residual",
    )(a1, a2, w_bf16[:k1], w_bf16[k1:], h, gate)


def _dft_cs(n, scale):
    k = np.arange(n)
    ang = 2.0 * np.pi * ((k[:, None] * k[None, :]) % n) / n
    return np.cos(ang) * scale, np.sin(ang) * scale


def _blockdiag(m, reps):
    n = m.shape[0]
    out = np.zeros((n * reps, n * reps), m.dtype)
    for r in range(reps):
        out[r * n:(r + 1) * n, r * n:(r + 1) * n] = m
    return out


def _fnet_lat_kernel(x_ref, cc_ref, sc_ref, gr_ref, gi_ref, c2_ref, s2_ref, o_ref, yr_ref, yi_ref, *, n1):
    x = x_ref[...].astype(BF16)
    yr_ref[...] = jnp.dot(x, cc_ref[...], preferred_element_type=F32)
    yi_ref[...] = jnp.dot(x, sc_ref[...], preferred_element_type=F32)

    def stage1(t2, carry):
        rows = pl.ds(t2, n1, stride=LANES)
        xr = yr_ref[rows, :].astype(BF16)
        xi = yi_ref[rows, :].astype(BF16)
        gr = gr_ref[t2]
        gi = gi_ref[t2]
        yr_ref[rows, :] = (jnp.dot(gr, xr, preferred_element_type=F32)
                           - jnp.dot(gi, xi, preferred_element_type=F32))
        yi_ref[rows, :] = (jnp.dot(gr, xi, preferred_element_type=F32)
                           + jnp.dot(gi, xr, preferred_element_type=F32))
        return carry

    lax.fori_loop(0, LANES, stage1, 0)

    def stage2(k1, carry):
        rows = pl.ds(pl.multiple_of(k1 * LANES, LANES), LANES)
        yr = yr_ref[rows, :].astype(BF16)
        yi = yi_ref[rows, :].astype(BF16)
        out = (jnp.dot(c2_ref[...], yr, preferred_element_type=F32)
               + jnp.dot(s2_ref[...], yi, preferred_element_type=F32))
        o_ref[pl.ds(k1, LANES, stride=n1), :] = out
        return carry

    lax.fori_loop(0, n1, stage2, 0)


def _fnet_lat(z, nb, l):
    nt = z.shape[0]
    n1 = l // LANES
    cg, sg = _dft_cs(FNET_GD, FNET_GD ** -0.5)
    cc = _blockdiag(cg, LANES // FNET_GD)
    sc = -_blockdiag(sg, LANES // FNET_GD)
    t2 = np.arange(LANES)[:, None, None]
    k1 = np.arange(n1)[None, :, None]
    t1 = np.arange(n1)[None, None, :]
    ang = 2.0 * np.pi * ((k1 * (t2 + LANES * t1)) % l) / l
    gr, gi = np.cos(ang), -np.sin(ang)
    c2, s2 = _dft_cs(LANES, l ** -0.5)
    bf = lambda a: jnp.asarray(a, F32).astype(BF16)
    const2 = lambda b, hf: (0, 0)
    const3 = lambda b, hf: (0, 0, 0)
    return pl.pallas_call(
        functools.partial(_fnet_lat_kernel, n1=n1),
        grid=(nb, FNET_W // LANES),
        in_specs=[pl.BlockSpec((l, LANES), lambda b, hf: (b, hf)),
                  pl.BlockSpec((LANES, LANES), const2), pl.BlockSpec((LANES, LANES), const2),
                  pl.BlockSpec((LANES, n1, n1), const3), pl.BlockSpec((LANES, n1, n1), const3),
                  pl.BlockSpec((LANES, LANES), const2), pl.BlockSpec((LANES, LANES), const2)],
        out_specs=pl.BlockSpec((l, LANES), lambda b, hf: (b, hf)),
        out_shape=jax.ShapeDtypeStruct((nt, FNET_W), F32),
        scratch_shapes=[pltpu.VMEM((l, LANES), F32), pltpu.VMEM((l, LANES), F32)],
        compiler_params=_cparams("parallel", "parallel"),
        name="fnet_latent",
    )(z, bf(cc), bf(sc), bf(gr), bf(gi), bf(c2), bf(s2))


def _fnet_ctx_kernel(x_ref, cc_ref, sc_ref, cl_ref, sl_ref, prev_ref, o_ref):
    del prev_ref
    x = x_ref[...].astype(BF16)
    xc = jnp.dot(x, cc_ref[...], preferred_element_type=F32).astype(BF16)
    xs = jnp.dot(x, sc_ref[...], preferred_element_type=F32).astype(BF16)
    o_ref[...] = (jnp.dot(cl_ref[...], xc, preferred_element_type=F32)
                  - jnp.dot(sl_ref[...], xs, preferred_element_type=F32))


def _fnet_ctx(z, fnet_out, nb, l, ctx):
    cg, sg = _dft_cs(FNET_GD, FNET_GD ** -0.5)
    cc = _blockdiag(cg, FNET_W // FNET_GD)
    sc = _blockdiag(sg, FNET_W // FNET_GD)
    cl, sl = _dft_cs(ctx, ctx ** -0.5)
    bf = lambda a: jnp.asarray(a, F32).astype(BF16)
    off = nb * l // ctx
    const = lambda b: (0, 0)
    return pl.pallas_call(
        _fnet_ctx_kernel,
        grid=(nb,),
        in_specs=[pl.BlockSpec((ctx, FNET_W), lambda b: (off + b, 0)),
                  pl.BlockSpec((FNET_W, FNET_W), const), pl.BlockSpec((FNET_W, FNET_W), const),
                  pl.BlockSpec((ctx, ctx), const), pl.BlockSpec((ctx, ctx), const),
                  pl.BlockSpec(memory_space=pl.ANY)],
        out_specs=pl.BlockSpec((ctx, FNET_W), lambda b: (off + b, 0)),
        out_shape=jax.ShapeDtypeStruct(fnet_out.shape, F32),
        input_output_aliases={5: 0},
        compiler_params=_cparams("parallel"),
        name="fnet_context",
    )(z, bf(cc), bf(sc), bf(cl), bf(sl), fnet_out)


def _s5_tables(lam_re, lam_im, log_step, b_re, b_im, c_re, c_im, d_skip):
    t = S5_T
    hp = lax.Precision.HIGHEST
    step = jnp.exp(log_step.astype(F32))[..., None]
    lr = lam_re.astype(F32) * step
    li = lam_im.astype(F32) * step
    k = jnp.arange(t + 1, dtype=F32)
    mag = jnp.exp(lr[..., None] * k)
    ang = li[..., None] * k
    pr, pi = mag * jnp.cos(ang), mag * jnp.sin(ang)
    ar, ai = pr[..., 1], pi[..., 1]
    den = lam_re.astype(F32) ** 2 + lam_im.astype(F32) ** 2
    qr = ((ar - 1.0) * lam_re + ai * lam_im) / den
    qi = (ai * lam_re - (ar - 1.0) * lam_im) / den
    br = qr[..., None] * b_re - qi[..., None] * b_im
    bi = qr[..., None] * b_im + qi[..., None] * b_re
    cr, ci = c_re.astype(F32), c_im.astype(F32)
    wr = pr[..., None] * br[..., None, :] - pi[..., None] * bi[..., None, :]
    wi = pr[..., None] * bi[..., None, :] + pi[..., None] * br[..., None, :]
    kern = (jnp.einsum('xgcn,xgnkd->xgkcd', cr, wr, precision=hp)
            - jnp.einsum('xgcn,xgnkd->xgkcd', ci, wi, precision=hp))
    tt = np.arange(t)
    lag_f = np.clip(tt[:, None] - tt[None, :], 0, None)
    lag_b = np.clip(tt[None, :] - tt[:, None], 0, None)
    kf = kern[0][:, lag_f]
    kb = kern[1][:, lag_b]
    diag = kern[0][:, 0] + kern[1][:, 0] + jax.vmap(jnp.diag)(d_skip.astype(F32))
    tri = (tt[:, None] - tt[None, :])[None, :, :, None, None]
    m = jnp.where(tri > 0, kf, jnp.where(tri < 0, kb, diag[:, None, None]))
    g = m.shape[0]
    mt = m.transpose(0, 2, 4, 1, 3).reshape(g, t * S5_C, t * S5_C)
    pf_r = wr[0][:, :, t - 1 - tt].transpose(0, 2, 3, 1).reshape(g, t * S5_C, S5_N)
    pf_i = wi[0][:, :, t - 1 - tt].transpose(0, 2, 3, 1).reshape(g, t * S5_C, S5_N)
    pb_r = wr[1][:, :, tt].transpose(0, 2, 3, 1).reshape(g, t * S5_C, S5_N)
    pb_i = wi[1][:, :, tt].transpose(0, 2, 3, 1).reshape(g, t * S5_C, S5_N)
    p = jnp.concatenate([pf_r, pf_i, pb_r, pb_i], axis=-1)
    def readout(x, idx):
        zr = cr[x][:, :, :, None] * pr[x][:, None, :, idx] - ci[x][:, :, :, None] * pi[x][:, None, :, idx]
        zi = cr[x][:, :, :, None] * pi[x][:, None, :, idx] + ci[x][:, :, :, None] * pr[x][:, None, :, idx]
        to_rows = lambda a: a.transpose(0, 2, 3, 1).reshape(g, S5_N, t * S5_C)
        return to_rows(zr), to_rows(-zi)
    qf_r, qf_i = readout(0, tt + 1)
    qb_r, qb_i = readout(1, t - tt)
    q = jnp.concatenate([qf_r, qf_i, qb_r, qb_i], axis=1)
    at_r, at_i = pr[..., t], pi[..., t]
    return mt.astype(BF16), p.astype(BF16), q.astype(BF16), at_r, at_i


def _s5_local_kernel(u_ref, p_ref, o_ref):
    o_ref[0] = jnp.dot(u_ref[0].astype(BF16), p_ref[0], preferred_element_type=F32)


def _s5_local(u, p):
    g, r, w = u.shape
    n4 = p.shape[2]
    return pl.pallas_call(
        _s5_local_kernel,
        grid=(g,),
        in_specs=[pl.BlockSpec((1, r, w), lambda i: (i, 0, 0)),
                  pl.BlockSpec((1, w, n4), lambda i: (i, 0, 0))],
        out_specs=pl.BlockSpec((1, r, n4), lambda i: (i, 0, 0)),
        out_shape=jax.ShapeDtypeStruct((g, r, n4), F32),
        compiler_params=_cparams("parallel"),
        name="s5_chunk_states",
    )(u, p)


def _s5_scan_kernel(sr_ref, si_ref, ar_ref, ai_ref, or_ref, oi_ref):
    ar, ai = ar_ref[...], ai_ref[...]

    def body(s, carry):
        hr, hi = carry
        or_ref[s] = hr
        oi_ref[s] = hi
        return (ar * hr - ai * hi + sr_ref[s], ar * hi + ai * hr + si_ref[s])

    zero = jnp.zeros(ar.shape, F32)
    lax.fori_loop(0, sr_ref.shape[0], body, (zero, zero))


def _s5_scan(sr, si, ar, ai):
    nch, rows, w = sr.shape
    wb = 512
    blk = pl.BlockSpec((nch, rows, wb), lambda i: (0, 0, i))
    cst = pl.BlockSpec((rows, wb), lambda i: (0, i))
    return pl.pallas_call(
        _s5_scan_kernel,
        grid=(w // wb,),
        in_specs=[blk, blk, cst, cst],
        out_specs=[blk, blk],
        out_shape=[jax.ShapeDtypeStruct(sr.shape, F32)] * 2,
        compiler_params=_cparams("parallel"),
        name="s5_chunk_scan",
    )(sr, si, ar, ai)


def _s5_apply_kernel(u_ref, mt_ref, h_ref, q_ref, o_ref):
    y = jnp.dot(u_ref[0].astype(BF16), mt_ref[0], preferred_element_type=F32)
    o_ref[0] = y + jnp.dot(h_ref[0].astype(BF16), q_ref[0], preferred_element_type=F32)


def _s5_apply(u, mt, hin, q):
    g, r, w = u.shape
    n4 = hin.shape[2]
    return pl.pallas_call(
        _s5_apply_kernel,
        grid=(g,),
        in_specs=[pl.BlockSpec((1, r, w), lambda i: (i, 0, 0)),
                  pl.BlockSpec((1, w, w), lambda i: (i, 0, 0)),
                  pl.BlockSpec((1, r, n4), lambda i: (i, 0, 0)),
                  pl.BlockSpec((1, n4, w), lambda i: (i, 0, 0))],
        out_specs=pl.BlockSpec((1, r, w), lambda i: (i, 0, 0)),
        out_shape=jax.ShapeDtypeStruct((g, r, w), F32),
        compiler_params=_cparams("parallel"),
        name="s5_apply",
    )(u, mt, hin, q)


def _glu_kernel(y_ref, w_ref, b_ref, o_ref):
    y = jax.nn.gelu(y_ref[...])
    gate = jnp.dot(y.astype(BF16), w_ref[...], preferred_element_type=F32) + b_ref[...]
    o_ref[...] = y * jax.nn.sigmoid(gate)


def _glu(y, w_bf16, b):
    nt, w = y.shape
    return pl.pallas_call(
        _glu_kernel,
        grid=(nt // TM,),
        in_specs=[pl.BlockSpec((TM, w), lambda i: (i, 0)),
                  pl.BlockSpec((w, w), lambda i: (0, 0)),
                  pl.BlockSpec((1, w), lambda i: (0, 0))],
        out_specs=pl.BlockSpec((TM, w), lambda i: (i, 0)),
        out_shape=jax.ShapeDtypeStruct((nt, w), F32),
        compiler_params=_cparams("parallel"),
        name="s5_glu",
    )(y, w_bf16, b.reshape(1, w))


def _s5_mix(z, nb, l, ctx, tables, glu_w, glu_b):
    mt, p, q, at_r, at_i = tables
    t = S5_T
    ncl, ncc = l // t, ctx // t
    nch = ncl + ncc
    u_lat = z[:nb * l, FNET_W:].reshape(nb, ncl, t, S5_G, S5_C)
    u_ctx = z[nb * l:, FNET_W:].reshape(nb, ncc, t, S5_G, S5_C)
    u = jnp.concatenate([u_ctx, u_lat], axis=1)
    u = u.transpose(3, 1, 0, 2, 4).reshape(S5_G, nch * nb, t * S5_C)
    s = _s5_local(u, p).reshape(S5_G, nch, nb, 4, S5_N)
    order_b = np.concatenate([np.arange(ncc - 1, -1, -1), np.arange(nch - 1, ncc - 1, -1)])
    inv_b = np.argsort(order_b)
    to_scan = lambda a: a.transpose(1, 2, 0, 3).reshape(nch, nb, S5_G * S5_N)
    pad = (-2 * nb) % SUBLANES
    def scan_in(j):
        a = jnp.concatenate([to_scan(s[:, :, :, j]), to_scan(s[:, order_b][:, :, :, 2 + j])], axis=1)
        return jnp.pad(a, ((0, 0), (0, pad), (0, 0)))
    def scan_const(a):
        rows = jnp.concatenate([jnp.broadcast_to(a[0].reshape(1, -1), (nb, S5_G * S5_N)),
                                jnp.broadcast_to(a[1].reshape(1, -1), (nb, S5_G * S5_N))], axis=0)
        return jnp.pad(rows, ((0, pad), (0, 0)))
    hr, hi = _s5_scan(scan_in(0), scan_in(1), scan_const(at_r), scan_const(at_i))
    from_scan = lambda a: a.reshape(nch, nb, S5_G, S5_N).transpose(2, 0, 1, 3)
    hin = jnp.stack([from_scan(hr[:, :nb]), from_scan(hi[:, :nb]),
                     from_scan(hr[:, nb:2 * nb])[:, inv_b], from_scan(hi[:, nb:2 * nb])[:, inv_b]], axis=3)
    hin = hin.reshape(S5_G, nch * nb, 4 * S5_N)
    y = _s5_apply(u, mt, hin, q).reshape(S5_G, nch, nb, t, S5_C).transpose(2, 1, 3, 0, 4)
    y_ctx = y[:, :ncc].reshape(nb * ctx, S5_W)
    y_lat = y[:, ncc:].reshape(nb * l, S5_W)
    return _glu(jnp.concatenate([y_lat, y_ctx], axis=0), glu_w.astype(BF16), glu_b)


def _sgu_kernel(z_ref, lng_ref, lnb_ref, ws_ref, bs_ref, o_ref):
    for ch in range(TM // SGU_CHUNK):
        rows = slice(ch * SGU_CHUNK, (ch + 1) * SGU_CHUNK)
        for g in range(SGU_GROUPS):
            cols = slice(g * LANES, (g + 1) * LANES)
            u = jax.nn.gelu(z_ref[rows, g * LANES:(g + 1) * LANES])
            v = jax.nn.gelu(z_ref[rows, SGU_W + g * LANES:SGU_W + (g + 1) * LANES])
            mu = jnp.mean(v, axis=-1, keepdims=True)
            vc = v - mu
            var = jnp.mean(vc * vc, axis=-1, keepdims=True)
            vn = vc * lax.rsqrt(var + EPS) * lng_ref[g:g + 1, :] + lnb_ref[g:g + 1, :]
            vm = jnp.dot(ws_ref[g], vn.astype(BF16), preferred_element_type=F32) + bs_ref[:, g:g + 1]
            o_ref[rows, cols] = u * vm


def _sgu(z, ln_g, ln_b, w_s, b_s):
    nt = z.shape[0]
    return pl.pallas_call(
        _sgu_kernel,
        grid=(nt // TM,),
        in_specs=[pl.BlockSpec((TM, 2 * SGU_W), lambda i: (i, 0)),
                  pl.BlockSpec((SGU_GROUPS, LANES), lambda i: (0, 0)),
                  pl.BlockSpec((SGU_GROUPS, LANES), lambda i: (0, 0)),
                  pl.BlockSpec((SGU_GROUPS, SGU_CHUNK, SGU_CHUNK), lambda i: (0, 0, 0)),
                  pl.BlockSpec((SGU_CHUNK, SGU_GROUPS), lambda i: (0, 0))],
        out_specs=pl.BlockSpec((TM, SGU_W), lambda i: (i, 0)),
        out_shape=jax.ShapeDtypeStruct((nt, SGU_W), F32),
        compiler_params=_cparams("parallel"),
        name="spatial_gating",
    )(z, ln_g, ln_b, w_s.astype(BF16), b_s.T)


def _rope_tables(l):
    t = np.arange(l)
    inv_freq = ROPE_BASE ** (-np.arange(0, ROPE_AXIS_DIM, 2, dtype=np.float32) / ROPE_AXIS_DIM)
    inv_freq = inv_freq.astype(np.float32)
    d = np.arange(LANES) % HEAD_DIM
    pos = np.where(d[None, :] < ROPE_AXIS_DIM, (t // GRID_W)[:, None], (t % GRID_W)[:, None]).astype(np.float32)
    ang = pos * inv_freq[d % (ROPE_AXIS_DIM // 2)][None, :]
    upper = (d % ROPE_AXIS_DIM) >= ROPE_AXIS_DIM // 2
    cos = np.cos(ang)
    sin_lo = np.where(upper[None, :], np.sin(ang), 0.0)
    sin_hi = np.where(upper[None, :], 0.0, -np.sin(ang))
    ident = lambda a, v: np.concatenate([a, np.full((TM, LANES), v)], axis=0).astype(np.float32)
    return jnp.asarray(ident(cos, 1.0)), jnp.asarray(ident(sin_lo, 0.0)), jnp.asarray(ident(sin_hi, 0.0))


def _qkv_kernel(q_ref, k_ref, v_ref, cos_ref, slo_ref, shi_ref, qo_ref, ko_ref, vo_ref):
    cos, slo, shi = cos_ref[...], slo_ref[...], shi_ref[...]
    half = ROPE_AXIS_DIM // 2
    for h in range(HEADS):
        cols = slice(h * LANES, (h + 1) * LANES)
        for src, dst, scale in ((q_ref, qo_ref, HEAD_DIM ** -0.5), (k_ref, ko_ref, 1.0)):
            x = src[:, cols]
            y = x * cos + pltpu.roll(x, half, 1) * slo + pltpu.roll(x, LANES - half, 1) * shi
            dst[:, cols] = (y * scale).astype(BF16)
    vo_ref[...] = v_ref[...].astype(BF16)


def _qkv(z, rope, geom):
    nt = z.shape[0]
    n_lat_tiles, tiles_per_batch, _ = geom
    tab = pl.BlockSpec((TM, LANES), lambda i: (jnp.where(i < n_lat_tiles, i % tiles_per_batch, tiles_per_batch), 0))
    col = lambda c: pl.BlockSpec((TM, QK_W), lambda i: (i, c))
    out = pl.BlockSpec((TM, QK_W), lambda i: (i, 0))
    return pl.pallas_call(
        _qkv_kernel,
        grid=(nt // TM,),
        in_specs=[col(Q_OFF // QK_W), col(KV_OFF // QK_W), col((KV_OFF + QK_W) // QK_W), tab, tab, tab],
        out_specs=[out, out, out],
        out_shape=[jax.ShapeDtypeStruct((nt, QK_W), BF16)] * 3,
        compiler_params=_cparams("parallel"),
        name="qkv_rope",
    )(z, z, z, *rope)


def _attn_kernel(lam_ref, q_ref, kc_ref, vc_ref, *rest, n_lat_chunks, out_scale):
    if n_lat_chunks:
        kl_ref, vl_ref, g_ref, o_ref = rest
    else:
        g_ref, _, o_ref = rest
    q = q_ref[...]
    lane = lax.broadcasted_iota(I32, q.shape, 1)
    zero = jnp.zeros_like(q)
    qs = (jnp.where(lane < HEAD_DIM, q, zero), jnp.where(lane >= HEAD_DIM, q, zero))
    nt_dims = (((1,), (1,)), ((), ()))

    def step(k, v, carry):
        out = []
        for j in range(2):
            m, l, a = carry[j]
            s = lax.dot_general(qs[j], k, nt_dims, preferred_element_type=F32)
            mn = jnp.maximum(m, jnp.max(s, axis=-1, keepdims=True))
            alpha = jnp.exp(m - mn)
            p = jnp.exp(s - mn)
            l = alpha * l + jnp.sum(p, axis=-1, keepdims=True)
            a = alpha * a + jnp.dot(p.astype(BF16), v, preferred_element_type=F32)
            out.append((mn, l, a))
        return tuple(out)

    rows = q.shape[0]
    init = tuple((jnp.full((rows, 1), NEG, F32), jnp.zeros((rows, 1), F32), jnp.zeros((rows, V_DIM), F32))
                 for _ in range(2))
    carry = step(kc_ref[...], vc_ref[...], init)
    if n_lat_chunks:
        def body(c, carry):
            ks = pl.ds(pl.multiple_of(c * TK, TK), TK)
            return step(kl_ref[ks, :], vl_ref[ks, :], carry)
        carry = lax.fori_loop(0, n_lat_chunks, body, carry)
    (_, l1, a1), (_, l2, a2) = carry
    o = a1 / l1 - lam_ref[0] * (a2 / l2)
    o_ref[...] = _rms(o, g_ref[...]) * out_scale


def _attention(q, k, v, lam, subln_g, nb, l, ctx, lam_init):
    nt = q.shape[0]
    g = subln_g.reshape(1, V_DIM)
    scale = 1.0 - lam_init
    smem = pl.BlockSpec(memory_space=pltpu.SMEM)
    off = nb * l // ctx
    ctx_blk = lambda f: pl.BlockSpec((ctx, LANES), f)
    gspec3 = pl.BlockSpec((1, V_DIM), lambda b, h, i: (0, 0))
    any_spec = pl.BlockSpec(memory_space=pl.ANY)
    nq = l // TQ
    lat = pl.pallas_call(
        functools.partial(_attn_kernel, n_lat_chunks=l // TK, out_scale=scale),
        grid=(nb, HEADS, nq),
        in_specs=[smem,
                  pl.BlockSpec((TQ, LANES), lambda b, h, i: (b * nq + i, h)),
                  ctx_blk(lambda b, h, i: (off + b, h)), ctx_blk(lambda b, h, i: (off + b, h)),
                  pl.BlockSpec((l, LANES), lambda b, h, i: (b, h)),
                  pl.BlockSpec((l, LANES), lambda b, h, i: (b, h)),
                  gspec3],
        out_specs=pl.BlockSpec((TQ, LANES), lambda b, h, i: (b * nq + i, h)),
        out_shape=jax.ShapeDtypeStruct((nt, HEADS * V_DIM), F32),
        compiler_params=_cparams("parallel", "parallel", "arbitrary"),
        name="diff_attention_latent",
    )
    ctx_call = pl.pallas_call(
        functools.partial(_attn_kernel, n_lat_chunks=0, out_scale=scale),
        grid=(nb, HEADS),
        in_specs=[smem,
                  ctx_blk(lambda b, h: (off + b, h)),
                  ctx_blk(lambda b, h: (off + b, h)), ctx_blk(lambda b, h: (off + b, h)),
                  pl.BlockSpec((1, V_DIM), lambda b, h: (0, 0)), any_spec],
        out_specs=ctx_blk(lambda b, h: (off + b, h)),
        out_shape=jax.ShapeDtypeStruct((nt, HEADS * V_DIM), F32),
        input_output_aliases={5: 0},
        compiler_params=_cparams("parallel", "parallel"),
        name="diff_attention_context",
    )
    return ctx_call(lam, q, k, v, g, lat(lam, q, k, v, k, v, g))


def _router_kernel(h_ref, g_ref, sh_ref, sc_ref, rw_ref, rb_ref, tri_ref, mrt_ref, meta_ref, cnt_ref, carry_ref):
    @pl.when(pl.program_id(0) == 0)
    def _():
        carry_ref[...] = jnp.zeros_like(carry_ref)

    m = _rms(h_ref[...], g_ref[...]) * (1.0 + sc_ref[...]) + sh_ref[...]
    for j in range(ROW_TILES):
        mrt_ref[pl.ds(j, TM, stride=ROW_TILES), :] = m[:, j * LANES:(j + 1) * LANES]
    logits = jnp.dot(m, rw_ref[...], preferred_element_type=F32, precision=lax.Precision.HIGHEST) + rb_ref[...]
    lane = lax.broadcasted_iota(I32, logits.shape, 1).astype(F32)
    multihot = jnp.zeros(logits.shape, F32)
    vals, idxs, sels = [], [], []
    for _ in range(TOP_K):
        mx = jnp.max(logits, axis=-1, keepdims=True)
        idx = jnp.min(jnp.where(logits == mx, lane, float(LANES)), axis=-1, keepdims=True)
        sel = lane == idx
        logits = jnp.where(sel, NEG, logits)
        multihot = jnp.where(sel, 1.0, multihot)
        vals.append(mx)
        idxs.append(idx)
        sels.append(sel)
    ex = [jnp.exp(v - vals[0]) for v in vals]
    den = ex[0] + ex[1] + ex[2] + ex[3]
    cum = jnp.dot(tri_ref[...], multihot.astype(BF16), preferred_element_type=F32) + carry_ref[0:1, :]
    meta = jnp.zeros(logits.shape, F32)
    for k in range(TOP_K):
        rank = jnp.sum(jnp.where(sels[k], cum, 0.0), axis=-1, keepdims=True)
        meta = jnp.where(lane == k, idxs[k], meta)
        meta = jnp.where(lane == TOP_K + k, rank, meta)
        meta = jnp.where(lane == 2 * TOP_K + k, ex[k] / den, meta)
    meta_ref[...] = meta
    carry_ref[...] = carry_ref[...] + jnp.sum(multihot, axis=0, keepdims=True)
    cnt_ref[...] = carry_ref[...]


def _router(h, g, shift, scale, router_w, router_b, geom):
    nt = h.shape[0]
    rw = jnp.pad(router_w.astype(F32), ((0, 0), (0, LANES - N_EXPERTS)))
    rb = jnp.pad(router_b.astype(F32), (0, LANES - N_EXPERTS), constant_values=-1.0e30).reshape(1, LANES)
    tri = jnp.asarray(np.tril(np.ones((TM, TM), np.float32), -1)).astype(BF16)
    return pl.pallas_call(
        _router_kernel,
        grid=(nt // TM,),
        in_specs=[pl.BlockSpec((TM, D), lambda i: (i, 0)),
                  pl.BlockSpec((1, D), lambda i: (0, 0)),
                  _mod_spec(geom), _mod_spec(geom),
                  pl.BlockSpec((D, LANES), lambda i: (0, 0)),
                  pl.BlockSpec((1, LANES), lambda i: (0, 0)),
                  pl.BlockSpec((TM, TM), lambda i: (0, 0))],
        out_specs=[pl.BlockSpec((TM * ROW_TILES, LANES), lambda i: (i, 0)),
                   pl.BlockSpec((TM, LANES), lambda i: (i, 0)),
                   pl.BlockSpec((SUBLANES, LANES), lambda i: (0, 0))],
        out_shape=[jax.ShapeDtypeStruct((nt * ROW_TILES, LANES), F32),
                   jax.ShapeDtypeStruct((nt, LANES), F32),
                   jax.ShapeDtypeStruct((SUBLANES, LANES), F32)],
        scratch_shapes=[pltpu.VMEM((SUBLANES, LANES), F32)],
        compiler_params=_cparams("arbitrary"),
        name="moe_router",
    )(h, g.reshape(1, D), shift, scale, rw, rb, tri)


def _dispatch_kernel(dest_ref, m_ref, zero_ref, x_ref, sem):
    del zero_ref

    def row_copy(r, dst_row):
        return pltpu.make_async_copy(
            m_ref.at[pl.ds(pl.multiple_of(r * ROW_TILES, ROW_TILES), ROW_TILES), :],
            x_ref.at[pl.ds(pl.multiple_of(dst_row * ROW_TILES, ROW_TILES), ROW_TILES), :], sem)

    def body(r, carry):
        for k in range(TOP_K):
            row_copy(r, dest_ref[0, r * TOP_K + k]).start()
        return carry

    lax.fori_loop(0, TM, body, 0)
    for _ in range(TOP_K):
        pltpu.make_async_copy(m_ref, x_ref.at[pl.ds(0, TM * ROW_TILES), :], sem).wait()


def _dispatch(dest, m_rt, n_rows):
    nt = m_rt.shape[0] // ROW_TILES
    n_tiles = nt // TM
    return pl.pallas_call(
        _dispatch_kernel,
        grid=(n_tiles,),
        in_specs=[pl.BlockSpec((None, 1, TM * TOP_K), lambda i: (i, 0, 0), memory_space=pltpu.SMEM),
                  pl.BlockSpec((TM * ROW_TILES, LANES), lambda i: (i, 0)),
                  pl.BlockSpec(memory_space=pl.ANY)],
        out_specs=pl.BlockSpec(memory_space=pl.ANY),
        out_shape=jax.ShapeDtypeStruct((n_rows * ROW_TILES, LANES), F32),
        scratch_shapes=[pltpu.SemaphoreType.DMA(())],
        input_output_aliases={2: 0},
        compiler_params=_cparams("arbitrary"),
        name="moe_dispatch",
    )(dest.reshape(n_tiles, 1, TM * TOP_K), m_rt, jnp.zeros((n_rows * ROW_TILES, LANES), F32))


def _expert_kernel(be_ref, nu_ref, x_ref, w1_ref, b1_ref, w2_ref, b2_ref, o_ref, w1b_ref, w2b_ref):
    b = pl.program_id(0)
    prev = be_ref[jnp.maximum(b - 1, 0)]
    fresh = jnp.logical_or(b == 0, be_ref[b] != prev)

    @pl.when(jnp.logical_and(fresh, b < nu_ref[0]))
    def _():
        w1b_ref[...] = w1_ref[0].astype(BF16)
        w2b_ref[...] = w2_ref[0].astype(BF16)

    @pl.when(b < nu_ref[0])
    def _():
        x = jnp.concatenate([x_ref[pl.ds(j, MOE_BLOCK, stride=ROW_TILES), :] for j in range(ROW_TILES)], axis=1)
        hh = jnp.dot(x.astype(BF16), w1b_ref[...], preferred_element_type=F32) + b1_ref[0]
        glu = jnp.minimum(hh[:, :D_EXPERT], SWIGLU_LIMIT)
        lin = jnp.clip(hh[:, D_EXPERT:], -SWIGLU_LIMIT, SWIGLU_LIMIT)
        act = glu * jax.nn.sigmoid(SWIGLU_ALPHA * glu) * (lin + 1.0)
        y = jnp.dot(act.astype(BF16), w2b_ref[...], preferred_element_type=F32) + b2_ref[0]
        for j in range(ROW_TILES):
            o_ref[pl.ds(j, MOE_BLOCK, stride=ROW_TILES), :] = y[:, j * LANES:(j + 1) * LANES]


def _experts(block_e, n_used, x_rt, w1, b1, w2, b2):
    n_blocks = block_e.shape[0]
    ne = w1.shape[0]
    blk = lambda b, be, nu: (jnp.minimum(b, nu[0] - 1), 0)
    exp3 = lambda b, be, nu: (be[jnp.minimum(b, nu[0] - 1)], 0, 0)
    grid_spec = pltpu.PrefetchScalarGridSpec(
        num_scalar_prefetch=2,
        grid=(n_blocks,),
        in_specs=[pl.BlockSpec((MOE_BLOCK * ROW_TILES, LANES), blk),
                  pl.BlockSpec((1, D, 2 * D_EXPERT), exp3),
                  pl.BlockSpec((1, 1, 2 * D_EXPERT), exp3),
                  pl.BlockSpec((1, D_EXPERT, D), exp3),
                  pl.BlockSpec((1, 1, D), exp3)],
        out_specs=pl.BlockSpec((MOE_BLOCK * ROW_TILES, LANES), blk),
        scratch_shapes=[pltpu.VMEM((D, 2 * D_EXPERT), BF16), pltpu.VMEM((D_EXPERT, D), BF16)],
    )
    return pl.pallas_call(
        _expert_kernel,
        grid_spec=grid_spec,
        out_shape=jax.ShapeDtypeStruct(x_rt.shape, F32),
        compiler_params=_cparams("arbitrary"),
        name="moe_experts",
    )(block_e, n_used, x_rt, w1, b1.reshape(ne, 1, -1), w2, b2.reshape(ne, 1, -1))


def _combine_kernel(dest_ref, y_ref, h_ref, meta_ref, gate_ref, fg_ref, o_ref, buf_ref, sem, *, final):
    def row_copy(k, r, src_row):
        return pltpu.make_async_copy(
            y_ref.at[pl.ds(pl.multiple_of(src_row * ROW_TILES, ROW_TILES), ROW_TILES), :],
            buf_ref.at[k, pl.ds(pl.multiple_of(r * ROW_TILES, ROW_TILES), ROW_TILES), :], sem.at[k])

    def body(r, carry):
        for k in range(TOP_K):
            row_copy(k, r, dest_ref[0, r * TOP_K + k]).start()
        return carry

    lax.fori_loop(0, TM, body, 0)
    for k in range(TOP_K):
        pltpu.make_async_copy(y_ref.at[pl.ds(0, TM * ROW_TILES), :], buf_ref.at[k], sem.at[k]).wait()
    meta = meta_ref[...]
    gates = [meta[:, 2 * TOP_K + k:2 * TOP_K + k + 1] for k in range(TOP_K)]
    outs = []
    for j in range(ROW_TILES):
        cols = slice(j * LANES, (j + 1) * LANES)
        f = gates[0] * buf_ref[0, pl.ds(j, TM, stride=ROW_TILES), :]
        for k in range(1, TOP_K):
            f = f + gates[k] * buf_ref[k, pl.ds(j, TM, stride=ROW_TILES), :]
        outs.append(h_ref[:, cols] + gate_ref[:, cols] * f)
    if final:
        hn = jnp.concatenate(outs, axis=1)
        o_ref[...] = _rms(hn, fg_ref[...])
    else:
        for j in range(ROW_TILES):
            o_ref[:, j * LANES:(j + 1) * LANES] = outs[j]


def _combine(dest, y_rt, h, meta, gate, final_g, geom, n_tiles, final):
    return pl.pallas_call(
        functools.partial(_combine_kernel, final=final),
        grid=(n_tiles,),
        in_specs=[pl.BlockSpec((None, 1, TM * TOP_K), lambda i: (i, 0, 0), memory_space=pltpu.SMEM),
                  pl.BlockSpec(memory_space=pl.ANY),
                  pl.BlockSpec((TM, D), lambda i: (i, 0)),
                  pl.BlockSpec((TM, LANES), lambda i: (i, 0)),
                  _mod_spec(geom),
                  pl.BlockSpec((1, D), lambda i: (0, 0))],
        out_specs=pl.BlockSpec((TM, D), lambda i: (i, 0)),
        out_shape=jax.ShapeDtypeStruct((n_tiles * TM, D), F32),
        scratch_shapes=[pltpu.VMEM((TOP_K, TM * ROW_TILES, LANES), F32), pltpu.SemaphoreType.DMA((TOP_K,))],
        compiler_params=_cparams("arbitrary"),
        name="moe_combine",
    )(dest.reshape(-1, 1, TM * TOP_K), y_rt, h, meta, gate, final_g.reshape(1, D))


def _moe(h, g, shift, scale, gate, router_w, router_b, w1, b1, w2, b2, final_g, geom, n_out_tiles, final):
    nt = h.shape[0]
    m_rt, meta, cnt = _router(h, g, shift, scale, router_w, router_b, geom)
    expert = meta[:, :TOP_K].astype(I32)
    rank = meta[:, TOP_K:2 * TOP_K].astype(I32)
    counts = cnt[0, :N_EXPERTS].astype(I32)
    padded = (counts + MOE_BLOCK - 1) // MOE_BLOCK * MOE_BLOCK
    pad_ends = jnp.cumsum(padded)
    pad_starts = pad_ends - padded
    dest = (pad_starts[expert] + rank).reshape(-1)
    n_blocks = nt * TOP_K // MOE_BLOCK + N_EXPERTS
    block_start = jnp.arange(n_blocks, dtype=I32) * MOE_BLOCK
    block_e = jnp.minimum(jnp.sum((pad_ends[None, :] <= block_start[:, None]).astype(I32), axis=1),
                          N_EXPERTS - 1).astype(I32)
    n_used = (pad_ends[-1:] // MOE_BLOCK).astype(I32)
    x_rt = _dispatch(dest, m_rt, n_blocks * MOE_BLOCK)
    y_rt = _experts(block_e, n_used, x_rt, w1, b1, w2, b2)
    return _combine(dest, y_rt, h, meta, gate, final_g, geom, n_out_tiles, final)


def kernel(x, c, ctx, c_ctx, ada_w, ada_b, norm1_g, norm2_g, ev_w_in, ev_w_out, s5_lam_re, s5_lam_im,
           s5_log_step, s5_b_re, s5_b_im, s5_c_re, s5_c_im, s5_d, s5_glu_w, s5_glu_b, od_w_in, od_w_out,
           sgu_ln_g, sgu_ln_b, sgu_w, sgu_b, diff_lq1, diff_lk1, diff_lq2, diff_lk2, diff_subln_g,
           router_w, router_b, exp_w1, exp_b1, exp_w2, exp_b2, final_g):
    nb, l, d = x.shape
    n_ctx = ctx.shape[1]
    depth = ada_w.shape[0]
    assert d == D and l % TM == 0 and n_ctx % TM == 0 and l % TK == 0
    assert TM % n_ctx == 0 or n_ctx % TM == 0
    n_lat = nb * l
    geom = (n_lat // TM, l // TM, nb)

    nbp = -(-(nb + 1) // SUBLANES) * SUBLANES
    cvec = jnp.zeros((nbp, D), F32).at[:nb].set(c).at[nb].set(c_ctx)
    mods = _adaln(cvec, ada_w, ada_b)
    mods = mods.reshape(depth, nbp, N_MOD, 1, D).transpose(0, 2, 1, 3, 4)

    h = jnp.concatenate([x.reshape(n_lat, D), ctx.reshape(nb * n_ctx, D)], axis=0)
    rope = _rope_tables(l)

    for i in range(depth):
        sh1, sc1, g1, sh2, sc2, g2 = (mods[i, j] for j in range(N_MOD))
        last = i == depth - 1
        j = i // 2
        if i % 2 == 0:
            z = _norm_in(h, norm1_g[i], sh1, sc1, ev_w_in[j].astype(BF16), geom)
            fnet = _fnet_ctx(z, _fnet_lat(z, nb, l), nb, l, n_ctx)
            tables = _s5_tables(s5_lam_re[j], s5_lam_im[j], s5_log_step[j], s5_b_re[j], s5_b_im[j],
                                s5_c_re[j], s5_c_im[j], s5_d[j])
            s5 = _s5_mix(z, nb, l, n_ctx, tables, s5_glu_w[j], s5_glu_b[j])
            h = _out_res(fnet, s5, ev_w_out[j].astype(BF16), h, g1, geom)
        else:
            lam_init = 0.8 - 0.6 * math.exp(-0.3 * i)
            z = _norm_in(h, norm1_g[i], sh1, sc1, od_w_in[j].astype(BF16), geom)
            sgu = _sgu(z, sgu_ln_g[j], sgu_ln_b[j], sgu_w[j], sgu_b[j])
            q, k, v = _qkv(z, rope, geom)
            lam = (jnp.exp(jnp.sum(diff_lq1[j].astype(F32) * diff_lk1[j].astype(F32)))
                   - jnp.exp(jnp.sum(diff_lq2[j].astype(F32) * diff_lk2[j].astype(F32))) + lam_init)
            attn = _attention(q, k, v, lam.reshape(1), diff_subln_g[j], nb, l, n_ctx, lam_init)
            h = _out_res(sgu, attn, od_w_out[j].astype(BF16), h, g1, geom)
        n_out_tiles = n_lat // TM if last else h.shape[0] // TM
        h = _moe(h, norm2_g[i], sh2, sc2, g2, router_w[i], router_b[i], exp_w1[i], exp_b1[i],
                 exp_w2[i], exp_b2[i], final_g, geom, n_out_tiles, last)
    return h.reshape(nb, l, D)
```

```python
import functools
import math

import numpy as np
import jax
import jax.numpy as jnp
from jax import lax
from jax.experimental import pallas as pl
from jax.experimental.pallas import tpu as pltpu

F32 = jnp.float32
BF16 = jnp.bfloat16
I32 = jnp.int32

D = 1024
N_MOD = 6
EPS = 1e-6
LANES = 128
SUBLANES = 8
ROW_TILES = D // LANES

FNET_W = 256
FNET_GD = 64
S5_W = 768
S5_C = 16
S5_G = S5_W // S5_C
S5_N = 64
S5_T = 64

SGU_W = 512
SGU_CHUNK = 128
SGU_GROUPS = 4
HEADS = 4
HEAD_DIM = 64
V_DIM = 128
QK_W = 512
Q_OFF = 2 * SGU_W
KV_OFF = Q_OFF + QK_W
ODD_IN = KV_OFF + QK_W + HEADS * V_DIM
GRID_W = 64
ROPE_AXIS_DIM = HEAD_DIM // 2
ROPE_BASE = 10000.0

N_EXPERTS = 32
TOP_K = 4
D_EXPERT = 1024
SWIGLU_ALPHA = 1.702
SWIGLU_LIMIT = 7.0
MOE_BLOCK = 256

TM = 256
TQ = 1024
TK = 512
NEG = -3.0e38
VMEM_LIMIT = 56 * 1024 * 1024


def _cparams(*sem):
    return pltpu.CompilerParams(dimension_semantics=sem, vmem_limit_bytes=VMEM_LIMIT)


def _rms(x, g):
    return x * lax.rsqrt(jnp.mean(x * x, axis=-1, keepdims=True) + EPS) * g


def _adaln_kernel(c_ref, w_ref, b_ref, o_ref):
    c = c_ref[...]
    a = (c * jax.nn.sigmoid(c)).astype(BF16)
    o_ref[0] = jnp.dot(a, w_ref[0].astype(BF16), preferred_element_type=F32) + b_ref[0]


def _adaln(cvec, ada_w, ada_b):
    depth, _, n = ada_w.shape
    tn = 1536
    nb = cvec.shape[0]
    return pl.pallas_call(
        _adaln_kernel,
        grid=(depth, n // tn),
        in_specs=[pl.BlockSpec((nb, D), lambda i, j: (0, 0)),
                  pl.BlockSpec((1, D, tn), lambda i, j: (i, 0, j)),
                  pl.BlockSpec((1, 1, tn), lambda i, j: (i, 0, j))],
        out_specs=pl.BlockSpec((1, nb, tn), lambda i, j: (i, 0, j)),
        out_shape=jax.ShapeDtypeStruct((depth, nb, n), F32),
        compiler_params=_cparams("parallel", "parallel"),
        name="adaln",
    )(cvec, ada_w, ada_b.reshape(depth, 1, n))


def _mod_spec(geom):
    n_lat_tiles, tiles_per_batch, nb = geom
    return pl.BlockSpec((None, 1, D),
                        lambda i: (jnp.where(i < n_lat_tiles, i // tiles_per_batch, nb), 0, 0))


def _norm_in_kernel(h_ref, g_ref, sh_ref, sc_ref, w_ref, o_ref):
    n = _rms(h_ref[...], g_ref[...]) * (1.0 + sc_ref[...]) + sh_ref[...]
    o_ref[...] = jnp.dot(n.astype(BF16), w_ref[...], preferred_element_type=F32)


def _norm_in(h, g, shift, scale, w_bf16, geom):
    nt = h.shape[0]
    n = w_bf16.shape[1]
    return pl.pallas_call(
        _norm_in_kernel,
        grid=(nt // TM,),
        in_specs=[pl.BlockSpec((TM, D), lambda i: (i, 0)),
                  pl.BlockSpec((1, D), lambda i: (0, 0)),
                  _mod_spec(geom), _mod_spec(geom),
                  pl.BlockSpec((D, n), lambda i: (0, 0))],
        out_specs=pl.BlockSpec((TM, n), lambda i: (i, 0)),
        out_shape=jax.ShapeDtypeStruct((nt, n), F32),
        compiler_params=_cparams("parallel"),
        name="norm_in_proj",
    )(h, g.reshape(1, D), shift, scale, w_bf16)


def _out_res_kernel(a1_ref, a2_ref, w1_ref, w2_ref, h_ref, gate_ref, o_ref):
    y = jnp.dot(a1_ref[...].astype(BF16), w1_ref[...], preferred_element_type=F32)
    y = y + jnp.dot(a2_ref[...].astype(BF16), w2_ref[...], preferred_element_type=F32)
    o_ref[...] = h_ref[...] + gate_ref[...] * y


def _out_res(a1, a2, w_bf16, h, gate, geom):
    nt = h.shape[0]
    k1, k2 = a1.shape[1], a2.shape[1]
    return pl.pallas_call(
        _out_res_kernel,
        grid=(nt // TM,),
        in_specs=[pl.BlockSpec((TM, k1), lambda i: (i, 0)),
                  pl.BlockSpec((TM, k2), lambda i: (i, 0)),
                  pl.BlockSpec((k1, D), lambda i: (0, 0)),
                  pl.BlockSpec((k2, D), lambda i: (0, 0)),
                  pl.BlockSpec((TM, D), lambda i: (i, 0)),
                  _mod_spec(geom)],
        out_specs=pl.BlockSpec((TM, D), lambda i: (i, 0)),
        out_shape=jax.ShapeDtypeStruct((nt, D), F32),
        compiler_params=_cparams("parallel"),
        name="out_proj_residual",
    )(a1, a2, w_bf16[:k1], w_bf16[k1:], h, gate)


def _dft_cs(n, scale):
    k = np.arange(n)
    ang = 2.0 * np.pi * ((k[:, None] * k[None, :]) % n) / n
    return np.cos(ang) * scale, np.sin(ang) * scale


def _blockdiag(m, reps):
    n = m.shape[0]
    out = np.zeros((n * reps, n * reps), m.dtype)
    for r in range(reps):
        out[r * n:(r + 1) * n, r * n:(r + 1) * n] = m
    return out


def _fnet_lat_kernel(x_ref, cc_ref, sc_ref, gr_ref, gi_ref, c2_ref, s2_ref, o_ref, yr_ref, yi_ref, *, n1):
    x = x_ref[...].astype(BF16)
    yr_ref[...] = jnp.dot(x, cc_ref[...], preferred_element_type=F32)
    yi_ref[...] = jnp.dot(x, sc_ref[...], preferred_element_type=F32)

    def stage1(t2, carry):
        rows = pl.ds(t2, n1, stride=LANES)
        xr = yr_ref[rows, :].astype(BF16)
        xi = yi_ref[rows, :].astype(BF16)
        gr = gr_ref[t2]
        gi = gi_ref[t2]
        yr_ref[rows, :] = (jnp.dot(gr, xr, preferred_element_type=F32)
                           - jnp.dot(gi, xi, preferred_element_type=F32))
        yi_ref[rows, :] = (jnp.dot(gr, xi, preferred_element_type=F32)
                           + jnp.dot(gi, xr, preferred_element_type=F32))
        return carry

    lax.fori_loop(0, LANES, stage1, 0)

    def stage2(k1, carry):
        rows = pl.ds(pl.multiple_of(k1 * LANES, LANES), LANES)
        yr = yr_ref[rows, :].astype(BF16)
        yi = yi_ref[rows, :].astype(BF16)
        out = (jnp.dot(c2_ref[...], yr, preferred_element_type=F32)
               + jnp.dot(s2_ref[...], yi, preferred_element_type=F32))
        o_ref[pl.ds(k1, LANES, stride=n1), :] = out
        return carry

    lax.fori_loop(0, n1, stage2, 0)


def _fnet_lat(z, nb, l):
    nt = z.shape[0]
    n1 = l // LANES
    cg, sg = _dft_cs(FNET_GD, FNET_GD ** -0.5)
    cc = _blockdiag(cg, LANES // FNET_GD)
    sc = -_blockdiag(sg, LANES // FNET_GD)
    t2 = np.arange(LANES)[:, None, None]
    k1 = np.arange(n1)[None, :, None]
    t1 = np.arange(n1)[None, None, :]
    ang = 2.0 * np.pi * ((k1 * (t2 + LANES * t1)) % l) / l
    gr, gi = np.cos(ang), -np.sin(ang)
    c2, s2 = _dft_cs(LANES, l ** -0.5)
    bf = lambda a: jnp.asarray(a, F32).astype(BF16)
    const2 = lambda b, hf: (0, 0)
    const3 = lambda b, hf: (0, 0, 0)
    return pl.pallas_call(
        functools.partial(_fnet_lat_kernel, n1=n1),
        grid=(nb, FNET_W // LANES),
        in_specs=[pl.BlockSpec((l, LANES), lambda b, hf: (b, hf)),
                  pl.BlockSpec((LANES, LANES), const2), pl.BlockSpec((LANES, LANES), const2),
                  pl.BlockSpec((LANES, n1, n1), const3), pl.BlockSpec((LANES, n1, n1), const3),
                  pl.BlockSpec((LANES, LANES), const2), pl.BlockSpec((LANES, LANES), const2)],
        out_specs=pl.BlockSpec((l, LANES), lambda b, hf: (b, hf)),
        out_shape=jax.ShapeDtypeStruct((nt, FNET_W), F32),
        scratch_shapes=[pltpu.VMEM((l, LANES), F32), pltpu.VMEM((l, LANES), F32)],
        compiler_params=_cparams("parallel", "parallel"),
        name="fnet_latent",
    )(z, bf(cc), bf(sc), bf(gr), bf(gi), bf(c2), bf(s2))


def _fnet_ctx_kernel(x_ref, cc_ref, sc_ref, cl_ref, sl_ref, prev_ref, o_ref):
    del prev_ref
    x = x_ref[...].astype(BF16)
    xc = jnp.dot(x, cc_ref[...], preferred_element_type=F32).astype(BF16)
    xs = jnp.dot(x, sc_ref[...], preferred_element_type=F32).astype(BF16)
    o_ref[...] = (jnp.dot(cl_ref[...], xc, preferred_element_type=F32)
                  - jnp.dot(sl_ref[...], xs, preferred_element_type=F32))


def _fnet_ctx(z, fnet_out, nb, l, ctx):
    cg, sg = _dft_cs(FNET_GD, FNET_GD ** -0.5)
    cc = _blockdiag(cg, FNET_W // FNET_GD)
    sc = _blockdiag(sg, FNET_W // FNET_GD)
    cl, sl = _dft_cs(ctx, ctx ** -0.5)
    bf = lambda a: jnp.asarray(a, F32).astype(BF16)
    off = nb * l // ctx
    const = lambda b: (0, 0)
    return pl.pallas_call(
        _fnet_ctx_kernel,
        grid=(nb,),
        in_specs=[pl.BlockSpec((ctx, FNET_W), lambda b: (off + b, 0)),
                  pl.BlockSpec((FNET_W, FNET_W), const), pl.BlockSpec((FNET_W, FNET_W), const),
                  pl.BlockSpec((ctx, ctx), const), pl.BlockSpec((ctx, ctx), const),
                  pl.BlockSpec(memory_space=pl.ANY)],
        out_specs=pl.BlockSpec((ctx, FNET_W), lambda b: (off + b, 0)),
        out_shape=jax.ShapeDtypeStruct(fnet_out.shape, F32),
        input_output_aliases={5: 0},
        compiler_params=_cparams("parallel"),
        name="fnet_context",
    )(z, bf(cc), bf(sc), bf(cl), bf(sl), fnet_out)


def _s5_tables(lam_re, lam_im, log_step, b_re, b_im, c_re, c_im, d_skip):
    t = S5_T
    hp = lax.Precision.HIGHEST
    step = jnp.exp(log_step.astype(F32))[..., None]
    lr = lam_re.astype(F32) * step
    li = lam_im.astype(F32) * step
    k = jnp.arange(t + 1, dtype=F32)
    mag = jnp.exp(lr[..., None] * k)
    ang = li[..., None] * k
    pr, pi = mag * jnp.cos(ang), mag * jnp.sin(ang)
    ar, ai = pr[..., 1], pi[..., 1]
    den = lam_re.astype(F32) ** 2 + lam_im.astype(F32) ** 2
    qr = ((ar - 1.0) * lam_re + ai * lam_im) / den
    qi = (ai * lam_re - (ar - 1.0) * lam_im) / den
    br = qr[..., None] * b_re - qi[..., None] * b_im
    bi = qr[..., None] * b_im + qi[..., None] * b_re
    cr, ci = c_re.astype(F32), c_im.astype(F32)
    wr = pr[..., None] * br[..., None, :] - pi[..., None] * bi[..., None, :]
    wi = pr[..., None] * bi[..., None, :] + pi[..., None] * br[..., None, :]
    kern = (jnp.einsum('xgcn,xgnkd->xgkcd', cr, wr, precision=hp)
            - jnp.einsum('xgcn,xgnkd->xgkcd', ci, wi, precision=hp))
    tt = np.arange(t)
    lag_f = np.clip(tt[:, None] - tt[None, :], 0, None)
    lag_b = np.clip(tt[None, :] - tt[:, None], 0, None)
    kf = kern[0][:, lag_f]
    kb = kern[1][:, lag_b]
    diag = kern[0][:, 0] + kern[1][:, 0] + jax.vmap(jnp.diag)(d_skip.astype(F32))
    tri = (tt[:, None] - tt[None, :])[None, :, :, None, None]
    m = jnp.where(tri > 0, kf, jnp.where(tri < 0, kb, diag[:, None, None]))
    g = m.shape[0]
    mt = m.transpose(0, 2, 4, 1, 3).reshape(g, t * S5_C, t * S5_C)
    pf_r = wr[0][:, :, t - 1 - tt].transpose(0, 2, 3, 1).reshape(g, t * S5_C, S5_N)
    pf_i = wi[0][:, :, t - 1 - tt].transpose(0, 2, 3, 1).reshape(g, t * S5_C, S5_N)
    pb_r = wr[1][:, :, tt].transpose(0, 2, 3, 1).reshape(g, t * S5_C, S5_N)
    pb_i = wi[1][:, :, tt].transpose(0, 2, 3, 1).reshape(g, t * S5_C, S5_N)
    p = jnp.concatenate([pf_r, pf_i, pb_r, pb_i], axis=-1)
    def readout(x, idx):
        zr = cr[x][:, :, :, None] * pr[x][:, None, :, idx] - ci[x][:, :, :, None] * pi[x][:, None, :, idx]
        zi = cr[x][:, :, :, None] * pi[x][:, None, :, idx] + ci[x][:, :, :, None] * pr[x][:, None, :, idx]
        to_rows = lambda a: a.transpose(0, 2, 3, 1).reshape(g, S5_N, t * S5_C)
        return to_rows(zr), to_rows(-zi)
    qf_r, qf_i = readout(0, tt + 1)
    qb_r, qb_i = readout(1, t - tt)
    q = jnp.concatenate([qf_r, qf_i, qb_r, qb_i], axis=1)
    at_r, at_i = pr[..., t], pi[..., t]
    return mt.astype(BF16), p.astype(BF16), q.astype(BF16), at_r, at_i


def _s5_local_kernel(u_ref, p_ref, o_ref):
    o_ref[0] = jnp.dot(u_ref[0].astype(BF16), p_ref[0], preferred_element_type=F32)


def _s5_local(u, p):
    g, r, w = u.shape
    n4 = p.shape[2]
    return pl.pallas_call(
        _s5_local_kernel,
        grid=(g,),
        in_specs=[pl.BlockSpec((1, r, w), lambda i: (i, 0, 0)),
                  pl.BlockSpec((1, w, n4), lambda i: (i, 0, 0))],
        out_specs=pl.BlockSpec((1, r, n4), lambda i: (i, 0, 0)),
        out_shape=jax.ShapeDtypeStruct((g, r, n4), F32),
        compiler_params=_cparams("parallel"),
        name="s5_chunk_states",
    )(u, p)


def _s5_scan_kernel(sr_ref, si_ref, ar_ref, ai_ref, or_ref, oi_ref):
    ar, ai = ar_ref[...], ai_ref[...]

    def body(s, carry):
        hr, hi = carry
        or_ref[s] = hr
        oi_ref[s] = hi
        return (ar * hr - ai * hi + sr_ref[s], ar * hi + ai * hr + si_ref[s])

    zero = jnp.zeros(ar.shape, F32)
    lax.fori_loop(0, sr_ref.shape[0], body, (zero, zero))


def _s5_scan(sr, si, ar, ai):
    nch, rows, w = sr.shape
    wb = 512
    blk = pl.BlockSpec((nch, rows, wb), lambda i: (0, 0, i))
    cst = pl.BlockSpec((rows, wb), lambda i: (0, i))
    return pl.pallas_call(
        _s5_scan_kernel,
        grid=(w // wb,),
        in_specs=[blk, blk, cst, cst],
        out_specs=[blk, blk],
        out_shape=[jax.ShapeDtypeStruct(sr.shape, F32)] * 2,
        compiler_params=_cparams("parallel"),
        name="s5_chunk_scan",
    )(sr, si, ar, ai)


def _s5_apply_kernel(u_ref, mt_ref, h_ref, q_ref, o_ref):
    y = jnp.dot(u_ref[0].astype(BF16), mt_ref[0], preferred_element_type=F32)
    o_ref[0] = y + jnp.dot(h_ref[0].astype(BF16), q_ref[0], preferred_element_type=F32)


def _s5_apply(u, mt, hin, q):
    g, r, w = u.shape
    n4 = hin.shape[2]
    return pl.pallas_call(
        _s5_apply_kernel,
        grid=(g,),
        in_specs=[pl.BlockSpec((1, r, w), lambda i: (i, 0, 0)),
                  pl.BlockSpec((1, w, w), lambda i: (i, 0, 0)),
                  pl.BlockSpec((1, r, n4), lambda i: (i, 0, 0)),
                  pl.BlockSpec((1, n4, w), lambda i: (i, 0, 0))],
        out_specs=pl.BlockSpec((1, r, w), lambda i: (i, 0, 0)),
        out_shape=jax.ShapeDtypeStruct((g, r, w), F32),
        compiler_params=_cparams("parallel"),
        name="s5_apply",
    )(u, mt, hin, q)


def _glu_kernel(y_ref, w_ref, b_ref, o_ref):
    y = jax.nn.gelu(y_ref[...])
    gate = jnp.dot(y.astype(BF16), w_ref[...], preferred_element_type=F32) + b_ref[...]
    o_ref[...] = y * jax.nn.sigmoid(gate)


def _glu(y, w_bf16, b):
    nt, w = y.shape
    return pl.pallas_call(
        _glu_kernel,
        grid=(nt // TM,),
        in_specs=[pl.BlockSpec((TM, w), lambda i: (i, 0)),
                  pl.BlockSpec((w, w), lambda i: (0, 0)),
                  pl.BlockSpec((1, w), lambda i: (0, 0))],
        out_specs=pl.BlockSpec((TM, w), lambda i: (i, 0)),
        out_shape=jax.ShapeDtypeStruct((nt, w), F32),
        compiler_params=_cparams("parallel"),
        name="s5_glu",
    )(y, w_bf16, b.reshape(1, w))


def _s5_mix(z, nb, l, ctx, tables, glu_w, glu_b):
    mt, p, q, at_r, at_i = tables
    t = S5_T
    ncl, ncc = l // t, ctx // t
    nch = ncl + ncc
    u_lat = z[:nb * l, FNET_W:].reshape(nb, ncl, t, S5_G, S5_C)
    u_ctx = z[nb * l:, FNET_W:].reshape(nb, ncc, t, S5_G, S5_C)
    u = jnp.concatenate([u_ctx, u_lat], axis=1)
    u = u.transpose(3, 1, 0, 2, 4).reshape(S5_G, nch * nb, t * S5_C)
    s = _s5_local(u, p).reshape(S5_G, nch, nb, 4, S5_N)
    order_b = np.concatenate([np.arange(ncc - 1, -1, -1), np.arange(nch - 1, ncc - 1, -1)])
    inv_b = np.argsort(order_b)
    to_scan = lambda a: a.transpose(1, 2, 0, 3).reshape(nch, nb, S5_G * S5_N)
    pad = (-2 * nb) % SUBLANES
    def scan_in(j):
        a = jnp.concatenate([to_scan(s[:, :, :, j]), to_scan(s[:, order_b][:, :, :, 2 + j])], axis=1)
        return jnp.pad(a, ((0, 0), (0, pad), (0, 0)))
    def scan_const(a):
        rows = jnp.concatenate([jnp.broadcast_to(a[0].reshape(1, -1), (nb, S5_G * S5_N)),
                                jnp.broadcast_to(a[1].reshape(1, -1), (nb, S5_G * S5_N))], axis=0)
        return jnp.pad(rows, ((0, pad), (0, 0)))
    hr, hi = _s5_scan(scan_in(0), scan_in(1), scan_const(at_r), scan_const(at_i))
    from_scan = lambda a: a.reshape(nch, nb, S5_G, S5_N).transpose(2, 0, 1, 3)
    hin = jnp.stack([from_scan(hr[:, :nb]), from_scan(hi[:, :nb]),
                     from_scan(hr[:, nb:2 * nb])[:, inv_b], from_scan(hi[:, nb:2 * nb])[:, inv_b]], axis=3)
    hin = hin.reshape(S5_G, nch * nb, 4 * S5_N)
    y = _s5_apply(u, mt, hin, q).reshape(S5_G, nch, nb, t, S5_C).transpose(2, 1, 3, 0, 4)
    y_ctx = y[:, :ncc].reshape(nb * ctx, S5_W)
    y_lat = y[:, ncc:].reshape(nb * l, S5_W)
    return _glu(jnp.concatenate([y_lat, y_ctx], axis=0), glu_w.astype(BF16), glu_b)


def _sgu_kernel(z_ref, lng_ref, lnb_ref, ws_ref, bs_ref, o_ref):
    for ch in range(TM // SGU_CHUNK):
        rows = slice(ch * SGU_CHUNK, (ch + 1) * SGU_CHUNK)
        for g in range(SGU_GROUPS):
            cols = slice(g * LANES, (g + 1) * LANES)
            u = jax.nn.gelu(z_ref[rows, g * LANES:(g + 1) * LANES])
            v = jax.nn.gelu(z_ref[rows, SGU_W + g * LANES:SGU_W + (g + 1) * LANES])
            mu = jnp.mean(v, axis=-1, keepdims=True)
            vc = v - mu
            var = jnp.mean(vc * vc, axis=-1, keepdims=True)
            vn = vc * lax.rsqrt(var + EPS) * lng_ref[g:g + 1, :] + lnb_ref[g:g + 1, :]
            vm = jnp.dot(ws_ref[g], vn.astype(BF16), preferred_element_type=F32) + bs_ref[:, g:g + 1]
            o_ref[rows, cols] = u * vm


def _sgu(z, ln_g, ln_b, w_s, b_s):
    nt = z.shape[0]
    return pl.pallas_call(
        _sgu_kernel,
        grid=(nt // TM,),
        in_specs=[pl.BlockSpec((TM, 2 * SGU_W), lambda i: (i, 0)),
                  pl.BlockSpec((SGU_GROUPS, LANES), lambda i: (0, 0)),
                  pl.BlockSpec((SGU_GROUPS, LANES), lambda i: (0, 0)),
                  pl.BlockSpec((SGU_GROUPS, SGU_CHUNK, SGU_CHUNK), lambda i: (0, 0, 0)),
                  pl.BlockSpec((SGU_CHUNK, SGU_GROUPS), lambda i: (0, 0))],
        out_specs=pl.BlockSpec((TM, SGU_W), lambda i: (i, 0)),
        out_shape=jax.ShapeDtypeStruct((nt, SGU_W), F32),
        compiler_params=_cparams("parallel"),
        name="spatial_gating",
    )(z, ln_g, ln_b, w_s.astype(BF16), b_s.T)


def _rope_tables(l):
    t = np.arange(l)
    inv_freq = ROPE_BASE ** (-np.arange(0, ROPE_AXIS_DIM, 2, dtype=np.float32) / ROPE_AXIS_DIM)
    inv_freq = inv_freq.astype(np.float32)
    d = np.arange(LANES) % HEAD_DIM
    pos = np.where(d[None, :] < ROPE_AXIS_DIM, (t // GRID_W)[:, None], (t % GRID_W)[:, None]).astype(np.float32)
    ang = pos * inv_freq[d % (ROPE_AXIS_DIM // 2)][None, :]
    upper = (d % ROPE_AXIS_DIM) >= ROPE_AXIS_DIM // 2
    cos = np.cos(ang)
    sin_lo = np.where(upper[None, :], np.sin(ang), 0.0)
    sin_hi = np.where(upper[None, :], 0.0, -np.sin(ang))
    ident = lambda a, v: np.concatenate([a, np.full((TM, LANES), v)], axis=0).astype(np.float32)
    return jnp.asarray(ident(cos, 1.0)), jnp.asarray(ident(sin_lo, 0.0)), jnp.asarray(ident(sin_hi, 0.0))


def _qkv_kernel(q_ref, k_ref, v_ref, cos_ref, slo_ref, shi_ref, qo_ref, ko_ref, vo_ref):
    cos, slo, shi = cos_ref[...], slo_ref[...], shi_ref[...]
    half = ROPE_AXIS_DIM // 2
    for h in range(HEADS):
        cols = slice(h * LANES, (h + 1) * LANES)
        for src, dst, scale in ((q_ref, qo_ref, HEAD_DIM ** -0.5), (k_ref, ko_ref, 1.0)):
            x = src[:, cols]
            y = x * cos + pltpu.roll(x, half, 1) * slo + pltpu.roll(x, LANES - half, 1) * shi
            dst[:, cols] = (y * scale).astype(BF16)
    vo_ref[...] = v_ref[...].astype(BF16)


def _qkv(z, rope, geom):
    nt = z.shape[0]
    n_lat_tiles, tiles_per_batch, _ = geom
    tab = pl.BlockSpec((TM, LANES), lambda i: (jnp.where(i < n_lat_tiles, i % tiles_per_batch, tiles_per_batch), 0))
    col = lambda c: pl.BlockSpec((TM, QK_W), lambda i: (i, c))
    out = pl.BlockSpec((TM, QK_W), lambda i: (i, 0))
    return pl.pallas_call(
        _qkv_kernel,
        grid=(nt // TM,),
        in_specs=[col(Q_OFF // QK_W), col(KV_OFF // QK_W), col((KV_OFF + QK_W) // QK_W), tab, tab, tab],
        out_specs=[out, out, out],
        out_shape=[jax.ShapeDtypeStruct((nt, QK_W), BF16)] * 3,
        compiler_params=_cparams("parallel"),
        name="qkv_rope",
    )(z, z, z, *rope)


def _attn_kernel(lam_ref, q_ref, kc_ref, vc_ref, *rest, n_lat_chunks, out_scale):
    if n_lat_chunks:
        kl_ref, vl_ref, g_ref, o_ref = rest
    else:
        g_ref, _, o_ref = rest
    q = q_ref[...]
    lane = lax.broadcasted_iota(I32, q.shape, 1)
    zero = jnp.zeros_like(q)
    qs = (jnp.where(lane < HEAD_DIM, q, zero), jnp.where(lane >= HEAD_DIM, q, zero))
    nt_dims = (((1,), (1,)), ((), ()))

    def step(k, v, carry):
        out = []
        for j in range(2):
            m, l, a = carry[j]
            s = lax.dot_general(qs[j], k, nt_dims, preferred_element_type=F32)
            mn = jnp.maximum(m, jnp.max(s, axis=-1, keepdims=True))
            alpha = jnp.exp(m - mn)
            p = jnp.exp(s - mn)
            l = alpha * l + jnp.sum(p, axis=-1, keepdims=True)
            a = alpha * a + jnp.dot(p.astype(BF16), v, preferred_element_type=F32)
            out.append((mn, l, a))
        return tuple(out)

    rows = q.shape[0]
    init = tuple((jnp.full((rows, 1), NEG, F32), jnp.zeros((rows, 1), F32), jnp.zeros((rows, V_DIM), F32))
                 for _ in range(2))
    carry = step(kc_ref[...], vc_ref[...], init)
    if n_lat_chunks:
        def body(c, carry):
            ks = pl.ds(pl.multiple_of(c * TK, TK), TK)
            return step(kl_ref[ks, :], vl_ref[ks, :], carry)
        carry = lax.fori_loop(0, n_lat_chunks, body, carry)
    (_, l1, a1), (_, l2, a2) = carry
    o = a1 / l1 - lam_ref[0] * (a2 / l2)
    o_ref[...] = _rms(o, g_ref[...]) * out_scale


def _attention(q, k, v, lam, subln_g, nb, l, ctx, lam_init):
    nt = q.shape[0]
    g = subln_g.reshape(1, V_DIM)
    scale = 1.0 - lam_init
    smem = pl.BlockSpec(memory_space=pltpu.SMEM)
    off = nb * l // ctx
    ctx_blk = lambda f: pl.BlockSpec((ctx, LANES), f)
    gspec3 = pl.BlockSpec((1, V_DIM), lambda b, h, i: (0, 0))
    any_spec = pl.BlockSpec(memory_space=pl.ANY)
    nq = l // TQ
    lat = pl.pallas_call(
        functools.partial(_attn_kernel, n_lat_chunks=l // TK, out_scale=scale),
        grid=(nb, HEADS, nq),
        in_specs=[smem,
                  pl.BlockSpec((TQ, LANES), lambda b, h, i: (b * nq + i, h)),
                  ctx_blk(lambda b, h, i: (off + b, h)), ctx_blk(lambda b, h, i: (off + b, h)),
                  pl.BlockSpec((l, LANES), lambda b, h, i: (b, h)),
                  pl.BlockSpec((l, LANES), lambda b, h, i: (b, h)),
                  gspec3],
        out_specs=pl.BlockSpec((TQ, LANES), lambda b, h, i: (b * nq + i, h)),
        out_shape=jax.ShapeDtypeStruct((nt, HEADS * V_DIM), F32),
        compiler_params=_cparams("parallel", "parallel", "arbitrary"),
        name="diff_attention_latent",
    )
    ctx_call = pl.pallas_call(
        functools.partial(_attn_kernel, n_lat_chunks=0, out_scale=scale),
        grid=(nb, HEADS),
        in_specs=[smem,
                  ctx_blk(lambda b, h: (off + b, h)),
                  ctx_blk(lambda b, h: (off + b, h)), ctx_blk(lambda b, h: (off + b, h)),
                  pl.BlockSpec((1, V_DIM), lambda b, h: (0, 0)), any_spec],
        out_specs=ctx_blk(lambda b, h: (off + b, h)),
        out_shape=jax.ShapeDtypeStruct((nt, HEADS * V_DIM), F32),
        input_output_aliases={5: 0},
        compiler_params=_cparams("parallel", "parallel"),
        name="diff_attention_context",
    )
    return ctx_call(lam, q, k, v, g, lat(lam, q, k, v, k, v, g))


def _router_kernel(h_ref, g_ref, sh_ref, sc_ref, rw_ref, rb_ref, tri_ref, mrt_ref, meta_ref, cnt_ref, carry_ref):
    @pl.when(pl.program_id(0) == 0)
    def _():
        carry_ref[...] = jnp.zeros_like(carry_ref)

    m = _rms(h_ref[...], g_ref[...]) * (1.0 + sc_ref[...]) + sh_ref[...]
    for j in range(ROW_TILES):
        mrt_ref[pl.ds(j, TM, stride=ROW_TILES), :] = m[:, j * LANES:(j + 1) * LANES]
    logits = jnp.dot(m, rw_ref[...], preferred_element_type=F32, precision=lax.Precision.HIGHEST) + rb_ref[...]
    lane = lax.broadcasted_iota(I32, logits.shape, 1).astype(F32)
    multihot = jnp.zeros(logits.shape, F32)
    vals, idxs, sels = [], [], []
    for _ in range(TOP_K):
        mx = jnp.max(logits, axis=-1, keepdims=True)
        idx = jnp.min(jnp.where(logits == mx, lane, float(LANES)), axis=-1, keepdims=True)
        sel = lane == idx
        logits = jnp.where(sel, NEG, logits)
        multihot = jnp.where(sel, 1.0, multihot)
        vals.append(mx)
        idxs.append(idx)
        sels.append(sel)
    ex = [jnp.exp(v - vals[0]) for v in vals]
    den = ex[0] + ex[1] + ex[2] + ex[3]
    cum = jnp.dot(tri_ref[...], multihot.astype(BF16), preferred_element_type=F32) + carry_ref[0:1, :]
    meta = jnp.zeros(logits.shape, F32)
    for k in range(TOP_K):
        rank = jnp.sum(jnp.where(sels[k], cum, 0.0), axis=-1, keepdims=True)
        meta = jnp.where(lane == k, idxs[k], meta)
        meta = jnp.where(lane == TOP_K + k, rank, meta)
        meta = jnp.where(lane == 2 * TOP_K + k, ex[k] / den, meta)
    meta_ref[...] = meta
    carry_ref[...] = carry_ref[...] + jnp.sum(multihot, axis=0, keepdims=True)
    cnt_ref[...] = carry_ref[...]


def _router(h, g, shift, scale, router_w, router_b, geom):
    nt = h.shape[0]
    rw = jnp.pad(router_w.astype(F32), ((0, 0), (0, LANES - N_EXPERTS)))
    rb = jnp.pad(router_b.astype(F32), (0, LANES - N_EXPERTS), constant_values=-1.0e30).reshape(1, LANES)
    tri = jnp.asarray(np.tril(np.ones((TM, TM), np.float32), -1)).astype(BF16)
    return pl.pallas_call(
        _router_kernel,
        grid=(nt // TM,),
        in_specs=[pl.BlockSpec((TM, D), lambda i: (i, 0)),
                  pl.BlockSpec((1, D), lambda i: (0, 0)),
                  _mod_spec(geom), _mod_spec(geom),
                  pl.BlockSpec((D, LANES), lambda i: (0, 0)),
                  pl.BlockSpec((1, LANES), lambda i: (0, 0)),
                  pl.BlockSpec((TM, TM), lambda i: (0, 0))],
        out_specs=[pl.BlockSpec((TM * ROW_TILES, LANES), lambda i: (i, 0)),
                   pl.BlockSpec((TM, LANES), lambda i: (i, 0)),
                   pl.BlockSpec((SUBLANES, LANES), lambda i: (0, 0))],
        out_shape=[jax.ShapeDtypeStruct((nt * ROW_TILES, LANES), F32),
                   jax.ShapeDtypeStruct((nt, LANES), F32),
                   jax.ShapeDtypeStruct((SUBLANES, LANES), F32)],
        scratch_shapes=[pltpu.VMEM((SUBLANES, LANES), F32)],
        compiler_params=_cparams("arbitrary"),
        name="moe_router",
    )(h, g.reshape(1, D), shift, scale, rw, rb, tri)


def _dispatch_kernel(dest_ref, m_ref, zero_ref, x_ref, sem):
    del zero_ref

    def row_copy(r, dst_row):
        return pltpu.make_async_copy(
            m_ref.at[pl.ds(pl.multiple_of(r * ROW_TILES, ROW_TILES), ROW_TILES), :],
            x_ref.at[pl.ds(pl.multiple_of(dst_row * ROW_TILES, ROW_TILES), ROW_TILES), :], sem)

    def body(r, carry):
        for k in range(TOP_K):
            row_copy(r, dest_ref[0, r * TOP_K + k]).start()
        return carry

    lax.fori_loop(0, TM, body, 0)
    for _ in range(TOP_K):
        pltpu.make_async_copy(m_ref, x_ref.at[pl.ds(0, TM * ROW_TILES), :], sem).wait()


def _dispatch(dest, m_rt, n_rows):
    nt = m_rt.shape[0] // ROW_TILES
    n_tiles = nt // TM
    return pl.pallas_call(
        _dispatch_kernel,
        grid=(n_tiles,),
        in_specs=[pl.BlockSpec((None, 1, TM * TOP_K), lambda i: (i, 0, 0), memory_space=pltpu.SMEM),
                  pl.BlockSpec((TM * ROW_TILES, LANES), lambda i: (i, 0)),
                  pl.BlockSpec(memory_space=pl.ANY)],
        out_specs=pl.BlockSpec(memory_space=pl.ANY),
        out_shape=jax.ShapeDtypeStruct((n_rows * ROW_TILES, LANES), F32),
        scratch_shapes=[pltpu.SemaphoreType.DMA(())],
        input_output_aliases={2: 0},
        compiler_params=_cparams("arbitrary"),
        name="moe_dispatch",
    )(dest.reshape(n_tiles, 1, TM * TOP_K), m_rt, jnp.zeros((n_rows * ROW_TILES, LANES), F32))


def _expert_kernel(be_ref, nu_ref, x_ref, w1_ref, b1_ref, w2_ref, b2_ref, o_ref, w1b_ref, w2b_ref):
    b = pl.program_id(0)
    prev = be_ref[jnp.maximum(b - 1, 0)]
    fresh = jnp.logical_or(b == 0, be_ref[b] != prev)

    @pl.when(jnp.logical_and(fresh, b < nu_ref[0]))
    def _():
        w1b_ref[...] = w1_ref[0].astype(BF16)
        w2b_ref[...] = w2_ref[0].astype(BF16)

    @pl.when(b < nu_ref[0])
    def _():
        x = jnp.concatenate([x_ref[pl.ds(j, MOE_BLOCK, stride=ROW_TILES), :] for j in range(ROW_TILES)], axis=1)
        hh = jnp.dot(x.astype(BF16), w1b_ref[...], preferred_element_type=F32) + b1_ref[0]
        glu = jnp.minimum(hh[:, :D_EXPERT], SWIGLU_LIMIT)
        lin = jnp.clip(hh[:, D_EXPERT:], -SWIGLU_LIMIT, SWIGLU_LIMIT)
        act = glu * jax.nn.sigmoid(SWIGLU_ALPHA * glu) * (lin + 1.0)
        y = jnp.dot(act.astype(BF16), w2b_ref[...], preferred_element_type=F32) + b2_ref[0]
        for j in range(ROW_TILES):
            o_ref[pl.ds(j, MOE_BLOCK, stride=ROW_TILES), :] = y[:, j * LANES:(j + 1) * LANES]


def _experts(block_e, n_used, x_rt, w1, b1, w2, b2):
    n_blocks = block_e.shape[0]
    ne = w1.shape[0]
    blk = lambda b, be, nu: (jnp.minimum(b, nu[0] - 1), 0)
    exp3 = lambda b, be, nu: (be[jnp.minimum(b, nu[0] - 1)], 0, 0)
    grid_spec = pltpu.PrefetchScalarGridSpec(
        num_scalar_prefetch=2,
        grid=(n_blocks,),
        in_specs=[pl.BlockSpec((MOE_BLOCK * ROW_TILES, LANES), blk),
                  pl.BlockSpec((1, D, 2 * D_EXPERT), exp3),
                  pl.BlockSpec((1, 1, 2 * D_EXPERT), exp3),
                  pl.BlockSpec((1, D_EXPERT, D), exp3),
                  pl.BlockSpec((1, 1, D), exp3)],
        out_specs=pl.BlockSpec((MOE_BLOCK * ROW_TILES, LANES), blk),
        scratch_shapes=[pltpu.VMEM((D, 2 * D_EXPERT), BF16), pltpu.VMEM((D_EXPERT, D), BF16)],
    )
    return pl.pallas_call(
        _expert_kernel,
        grid_spec=grid_spec,
        out_shape=jax.ShapeDtypeStruct(x_rt.shape, F32),
        compiler_params=_cparams("arbitrary"),
        name="moe_experts",
    )(block_e, n_used, x_rt, w1, b1.reshape(ne, 1, -1), w2, b2.reshape(ne, 1, -1))


def _combine_kernel(dest_ref, y_ref, h_ref, meta_ref, gate_ref, fg_ref, o_ref, buf_ref, sem, *, final):
    def row_copy(k, r, src_row):
        return pltpu.make_async_copy(
            y_ref.at[pl.ds(pl.multiple_of(src_row * ROW_TILES, ROW_TILES), ROW_TILES), :],
            buf_ref.at[k, pl.ds(pl.multiple_of(r * ROW_TILES, ROW_TILES), ROW_TILES), :], sem.at[k])

    def body(r, carry):
        for k in range(TOP_K):
            row_copy(k, r, dest_ref[0, r * TOP_K + k]).start()
        return carry

    lax.fori_loop(0, TM, body, 0)
    for k in range(TOP_K):
        pltpu.make_async_copy(y_ref.at[pl.ds(0, TM * ROW_TILES), :], buf_ref.at[k], sem.at[k]).wait()
    meta = meta_ref[...]
    gates = [meta[:, 2 * TOP_K + k:2 * TOP_K + k + 1] for k in range(TOP_K)]
    outs = []
    for j in range(ROW_TILES):
        cols = slice(j * LANES, (j + 1) * LANES)
        f = gates[0] * buf_ref[0, pl.ds(j, TM, stride=ROW_TILES), :]
        for k in range(1, TOP_K):
            f = f + gates[k] * buf_ref[k, pl.ds(j, TM, stride=ROW_TILES), :]
        outs.append(h_ref[:, cols] + gate_ref[:, cols] * f)
    if final:
        hn = jnp.concatenate(outs, axis=1)
        o_ref[...] = _rms(hn, fg_ref[...])
    else:
        for j in range(ROW_TILES):
            o_ref[:, j * LANES:(j + 1) * LANES] = outs[j]


def _combine(dest, y_rt, h, meta, gate, final_g, geom, n_tiles, final):
    return pl.pallas_call(
        functools.partial(_combine_kernel, final=final),
        grid=(n_tiles,),
        in_specs=[pl.BlockSpec((None, 1, TM * TOP_K), lambda i: (i, 0, 0), memory_space=pltpu.SMEM),
                  pl.BlockSpec(memory_space=pl.ANY),
                  pl.BlockSpec((TM, D), lambda i: (i, 0)),
                  pl.BlockSpec((TM, LANES), lambda i: (i, 0)),
                  _mod_spec(geom),
                  pl.BlockSpec((1, D), lambda i: (0, 0))],
        out_specs=pl.BlockSpec((TM, D), lambda i: (i, 0)),
        out_shape=jax.ShapeDtypeStruct((n_tiles * TM, D), F32),
        scratch_shapes=[pltpu.VMEM((TOP_K, TM * ROW_TILES, LANES), F32), pltpu.SemaphoreType.DMA((TOP_K,))],
        compiler_params=_cparams("arbitrary"),
        name="moe_combine",
    )(dest.reshape(-1, 1, TM * TOP_K), y_rt, h, meta, gate, final_g.reshape(1, D))


def _moe(h, g, shift, scale, gate, router_w, router_b, w1, b1, w2, b2, final_g, geom, n_out_tiles, final):
    nt = h.shape[0]
    m_rt, meta, cnt = _router(h, g, shift, scale, router_w, router_b, geom)
    expert = meta[:, :TOP_K].astype(I32)
    rank = meta[:, TOP_K:2 * TOP_K].astype(I32)
    counts = cnt[0, :N_EXPERTS].astype(I32)
    padded = (counts + MOE_BLOCK - 1) // MOE_BLOCK * MOE_BLOCK
    pad_ends = jnp.cumsum(padded)
    pad_starts = pad_ends - padded
    dest = (pad_starts[expert] + rank).reshape(-1)
    n_blocks = nt * TOP_K // MOE_BLOCK + N_EXPERTS
    block_start = jnp.arange(n_blocks, dtype=I32) * MOE_BLOCK
    block_e = jnp.minimum(jnp.sum((pad_ends[None, :] <= block_start[:, None]).astype(I32), axis=1),
                          N_EXPERTS - 1).astype(I32)
    n_used = (pad_ends[-1:] // MOE_BLOCK).astype(I32)
    x_rt = _dispatch(dest, m_rt, n_blocks * MOE_BLOCK)
    y_rt = _experts(block_e, n_used, x_rt, w1, b1, w2, b2)
    return _combine(dest, y_rt, h, meta, gate, final_g, geom, n_out_tiles, final)


def kernel(x, c, ctx, c_ctx, ada_w, ada_b, norm1_g, norm2_g, ev_w_in, ev_w_out, s5_lam_re, s5_lam_im,
           s5_log_step, s5_b_re, s5_b_im, s5_c_re, s5_c_im, s5_d, s5_glu_w, s5_glu_b, od_w_in, od_w_out,
           sgu_ln_g, sgu_ln_b, sgu_w, sgu_b, diff_lq1, diff_lk1, diff_lq2, diff_lk2, diff_subln_g,
           router_w, router_b, exp_w1, exp_b1, exp_w2, exp_b2, final_g):
    nb, l, d = x.shape
    n_ctx = ctx.shape[1]
    depth = ada_w.shape[0]
    assert d == D and l % TM == 0 and n_ctx % TM == 0 and l % TK == 0
    assert TM % n_ctx == 0 or n_ctx % TM == 0
    n_lat = nb * l
    geom = (n_lat // TM, l // TM, nb)

    nbp = -(-(nb + 1) // SUBLANES) * SUBLANES
    cvec = jnp.zeros((nbp, D), F32).at[:nb].set(c).at[nb].set(c_ctx)
    mods = _adaln(cvec, ada_w, ada_b)
    mods = mods.reshape(depth, nbp, N_MOD, 1, D).transpose(0, 2, 1, 3, 4)

    h = jnp.concatenate([x.reshape(n_lat, D), ctx.reshape(nb * n_ctx, D)], axis=0)
    rope = _rope_tables(l)

    for i in range(depth):
        sh1, sc1, g1, sh2, sc2, g2 = (mods[i, j] for j in range(N_MOD))
        last = i == depth - 1
        j = i // 2
        if i % 2 == 0:
            z = _norm_in(h, norm1_g[i], sh1, sc1, ev_w_in[j].astype(BF16), geom)
            fnet = _fnet_ctx(z, _fnet_lat(z, nb, l), nb, l, n_ctx)
            tables = _s5_tables(s5_lam_re[j], s5_lam_im[j], s5_log_step[j], s5_b_re[j], s5_b_im[j],
                                s5_c_re[j], s5_c_im[j], s5_d[j])
            s5 = _s5_mix(z, nb, l, n_ctx, tables, s5_glu_w[j], s5_glu_b[j])
            h = _out_res(fnet, s5, ev_w_out[j].astype(BF16), h, g1, geom)
        else:
            lam_init = 0.8 - 0.6 * math.exp(-0.3 * i)
            z = _norm_in(h, norm1_g[i], sh1, sc1, od_w_in[j].astype(BF16), geom)
            sgu = _sgu(z, sgu_ln_g[j], sgu_ln_b[j], sgu_w[j], sgu_b[j])
            q, k, v = _qkv(z, rope, geom)
            lam = (jnp.exp(jnp.sum(diff_lq1[j].astype(F32) * diff_lk1[j].astype(F32)))
                   - jnp.exp(jnp.sum(diff_lq2[j].astype(F32) * diff_lk2[j].astype(F32))) + lam_init)
            attn = _attention(q, k, v, lam.reshape(1), diff_subln_g[j], nb, l, n_ctx, lam_init)
            h = _out_res(sgu, attn, od_w_out[j].astype(BF16), h, g1, geom)
        n_out_tiles = n_lat // TM if last else h.shape[0] // TM
        h = _moe(h, norm2_g[i], sh2, sc2, g2, router_w[i], router_b[i], exp_w1[i], exp_b1[i],
                 exp_w2[i], exp_b2[i], final_g, geom, n_out_tiles, last)
    return h.reshape(nb, l, D)
```

```python
import functools
import math

import numpy as np
import jax
import jax.numpy as jnp
from jax import lax
from jax.experimental import pallas as pl
from jax.experimental.pallas import tpu as pltpu

F32 = jnp.float32
BF16 = jnp.bfloat16
I32 = jnp.int32

D = 1024
N_MOD = 6
EPS = 1e-6
LANES = 128
SUBLANES = 8
ROW_TILES = D // LANES

FNET_W = 256
FNET_GD = 64
S5_W = 768
S5_C = 16
S5_G = S5_W // S5_C
S5_N = 64
S5_T = 64

SGU_W = 512
SGU_CHUNK = 128
SGU_GROUPS = 4
HEADS = 4
HEAD_DIM = 64
V_DIM = 128
QK_W = 512
Q_OFF = 2 * SGU_W
KV_OFF = Q_OFF + QK_W
ODD_IN = KV_OFF + QK_W + HEADS * V_DIM
GRID_W = 64
ROPE_AXIS_DIM = HEAD_DIM // 2
ROPE_BASE = 10000.0

N_EXPERTS = 32
TOP_K = 4
D_EXPERT = 1024
SWIGLU_ALPHA = 1.702
SWIGLU_LIMIT = 7.0
MOE_BLOCK = 256

TM = 256
TQ = 1024
TK = 512
NEG = -3.0e38
VMEM_LIMIT = 56 * 1024 * 1024


def _cparams(*sem):
    return pltpu.CompilerParams(dimension_semantics=sem, vmem_limit_bytes=VMEM_LIMIT)


def _rms(x, g):
    return x * lax.rsqrt(jnp.mean(x * x, axis=-1, keepdims=True) + EPS) * g


def _adaln_kernel(c_ref, w_ref, b_ref, o_ref):
    c = c_ref[...]
    a = (c * jax.nn.sigmoid(c)).astype(BF16)
    o_ref[0] = jnp.dot(a, w_ref[0].astype(BF16), preferred_element_type=F32) + b_ref[0]


def _adaln(cvec, ada_w, ada_b):
    depth, _, n = ada_w.shape
    tn = 1536
    nb = cvec.shape[0]
    return pl.pallas_call(
        _adaln_kernel,
        grid=(depth, n // tn),
        in_specs=[pl.BlockSpec((nb, D), lambda i, j: (0, 0)),
                  pl.BlockSpec((1, D, tn), lambda i, j: (i, 0, j)),
                  pl.BlockSpec((1, 1, tn), lambda i, j: (i, 0, j))],
        out_specs=pl.BlockSpec((1, nb, tn), lambda i, j: (i, 0, j)),
        out_shape=jax.ShapeDtypeStruct((depth, nb, n), F32),
        compiler_params=_cparams("parallel", "parallel"),
        name="adaln",
    )(cvec, ada_w, ada_b.reshape(depth, 1, n))


def _mod_spec(geom):
    n_lat_tiles, tiles_per_batch, nb = geom
    return pl.BlockSpec((None, 1, D),
                        lambda i: (jnp.where(i < n_lat_tiles, i // tiles_per_batch, nb), 0, 0))


def _norm_in_kernel(h_ref, g_ref, sh_ref, sc_ref, w_ref, o_ref):
    n = _rms(h_ref[...], g_ref[...]) * (1.0 + sc_ref[...]) + sh_ref[...]
    o_ref[...] = jnp.dot(n.astype(BF16), w_ref[...], preferred_element_type=F32)


def _norm_in(h, g, shift, scale, w_bf16, geom):
    nt = h.shape[0]
    n = w_bf16.shape[1]
    return pl.pallas_call(
        _norm_in_kernel,
        grid=(nt // TM,),
        in_specs=[pl.BlockSpec((TM, D), lambda i: (i, 0)),
                  pl.BlockSpec((1, D), lambda i: (0, 0)),
                  _mod_spec(geom), _mod_spec(geom),
                  pl.BlockSpec((D, n), lambda i: (0, 0))],
        out_specs=pl.BlockSpec((TM, n), lambda i: (i, 0)),
        out_shape=jax.ShapeDtypeStruct((nt, n), F32),
        compiler_params=_cparams("parallel"),
        name="norm_in_proj",
    )(h, g.reshape(1, D), shift, scale, w_bf16)


def _out_res_kernel(a1_ref, a2_ref, w1_ref, w2_ref, h_ref, gate_ref, o_ref):
    y = jnp.dot(a1_ref[...].astype(BF16), w1_ref[...], preferred_element_type=F32)
    y = y + jnp.dot(a2_ref[...].astype(BF16), w2_ref[...], preferred_element_type=F32)
    o_ref[...] = h_ref[...] + gate_ref[...] * y


def _out_res(a1, a2, w_bf16, h, gate, geom):
    nt = h.shape[0]
    k1, k2 = a1.shape[1], a2.shape[1]
    return pl.pallas_call(
        _out_res_kernel,
        grid=(nt // TM,),
        in_specs=[pl.BlockSpec((TM, k1), lambda i: (i, 0)),
                  pl.BlockSpec((TM, k2), lambda i: (i, 0)),
                  pl.BlockSpec((k1, D), lambda i: (0, 0)),
                  pl.BlockSpec((k2, D), lambda i: (0, 0)),
                  pl.BlockSpec((TM, D), lambda i: (i, 0)),
                  _mod_spec(geom)],
        out_specs=pl.BlockSpec((TM, D), lambda i: (i, 0)),
        out_shape=jax.ShapeDtypeStruct((nt, D), F32),
        compiler_params=_cparams("parallel"),
        name="out_proj_residual",
    )(a1, a2, w_bf16[:k1], w_bf16[k1:], h, gate)


def _dft_cs(n, scale):
    k = np.arange(n)
    ang = 2.0 * np.pi * ((k[:, None] * k[None, :]) % n) / n
    return np.cos(ang) * scale, np.sin(ang) * scale


def _blockdiag(m, reps):
    n = m.shape[0]
    out = np.zeros((n * reps, n * reps), m.dtype)
    for r in range(reps):
        out[r * n:(r + 1) * n, r * n:(r + 1) * n] = m
    return out


def _fnet_lat_kernel(x_ref, cc_ref, sc_ref, gr_ref, gi_ref, c2_ref, s2_ref, o_ref, yr_ref, yi_ref, *, n1):
    x = x_ref[...].astype(BF16)
    yr_ref[...] = jnp.dot(x, cc_ref[...], preferred_element_type=F32)
    yi_ref[...] = jnp.dot(x, sc_ref[...], preferred_element_type=F32)

    def stage1(t2, carry):
        rows = pl.ds(t2, n1, stride=LANES)
        xr = yr_ref[rows, :].astype(BF16)
        xi = yi_ref[rows, :].astype(BF16)
        gr = gr_ref[t2]
        gi = gi_ref[t2]
        yr_ref[rows, :] = (jnp.dot(gr, xr, preferred_element_type=F32)
                           - jnp.dot(gi, xi, preferred_element_type=F32))
        yi_ref[rows, :] = (jnp.dot(gr, xi, preferred_element_type=F32)
                           + jnp.dot(gi, xr, preferred_element_type=F32))
        return carry

    lax.fori_loop(0, LANES, stage1, 0)

    def stage2(k1, carry):
        rows = pl.ds(pl.multiple_of(k1 * LANES, LANES), LANES)
        yr = yr_ref[rows, :].astype(BF16)
        yi = yi_ref[rows, :].astype(BF16)
        out = (jnp.dot(c2_ref[...], yr, preferred_element_type=F32)
               + jnp.dot(s2_ref[...], yi, preferred_element_type=F32))
        o_ref[pl.ds(k1, LANES, stride=n1), :] = out
        return carry

    lax.fori_loop(0, n1, stage2, 0)


def _fnet_lat(z, nb, l):
    nt = z.shape[0]
    n1 = l // LANES
    cg, sg = _dft_cs(FNET_GD, FNET_GD ** -0.5)
    cc = _blockdiag(cg, LANES // FNET_GD)
    sc = -_blockdiag(sg, LANES // FNET_GD)
    t2 = np.arange(LANES)[:, None, None]
    k1 = np.arange(n1)[None, :, None]
    t1 = np.arange(n1)[None, None, :]
    ang = 2.0 * np.pi * ((k1 * (t2 + LANES * t1)) % l) / l
    gr, gi = np.cos(ang), -np.sin(ang)
    c2, s2 = _dft_cs(LANES, l ** -0.5)
    bf = lambda a: jnp.asarray(a, F32).astype(BF16)
    const2 = lambda b, hf: (0, 0)
    const3 = lambda b, hf: (0, 0, 0)
    return pl.pallas_call(
        functools.partial(_fnet_lat_kernel, n1=n1),
        grid=(nb, FNET_W // LANES),
        in_specs=[pl.BlockSpec((l, LANES), lambda b, hf: (b, hf)),
                  pl.BlockSpec((LANES, LANES), const2), pl.BlockSpec((LANES, LANES), const2),
                  pl.BlockSpec((LANES, n1, n1), const3), pl.BlockSpec((LANES, n1, n1), const3),
                  pl.BlockSpec((LANES, LANES), const2), pl.BlockSpec((LANES, LANES), const2)],
        out_specs=pl.BlockSpec((l, LANES), lambda b, hf: (b, hf)),
        out_shape=jax.ShapeDtypeStruct((nt, FNET_W), F32),
        scratch_shapes=[pltpu.VMEM((l, LANES), F32), pltpu.VMEM((l, LANES), F32)],
        compiler_params=_cparams("parallel", "parallel"),
        name="fnet_latent",
    )(z, bf(cc), bf(sc), bf(gr), bf(gi), bf(c2), bf(s2))


def _fnet_ctx_kernel(x_ref, cc_ref, sc_ref, cl_ref, sl_ref, prev_ref, o_ref):
    del prev_ref
    x = x_ref[...].astype(BF16)
    xc = jnp.dot(x, cc_ref[...], preferred_element_type=F32).astype(BF16)
    xs = jnp.dot(x, sc_ref[...], preferred_element_type=F32).astype(BF16)
    o_ref[...] = (jnp.dot(cl_ref[...], xc, preferred_element_type=F32)
                  - jnp.dot(sl_ref[...], xs, preferred_element_type=F32))


def _fnet_ctx(z, fnet_out, nb, l, ctx):
    cg, sg = _dft_cs(FNET_GD, FNET_GD ** -0.5)
    cc = _blockdiag(cg, FNET_W // FNET_GD)
    sc = _blockdiag(sg, FNET_W // FNET_GD)
    cl, sl = _dft_cs(ctx, ctx ** -0.5)
    bf = lambda a: jnp.asarray(a, F32).astype(BF16)
    off = nb * l // ctx
    const = lambda b: (0, 0)
    return pl.pallas_call(
        _fnet_ctx_kernel,
        grid=(nb,),
        in_specs=[pl.BlockSpec((ctx, FNET_W), lambda b: (off + b, 0)),
                  pl.BlockSpec((FNET_W, FNET_W), const), pl.BlockSpec((FNET_W, FNET_W), const),
                  pl.BlockSpec((ctx, ctx), const), pl.BlockSpec((ctx, ctx), const),
                  pl.BlockSpec(memory_space=pl.ANY)],
        out_specs=pl.BlockSpec((ctx, FNET_W), lambda b: (off + b, 0)),
        out_shape=jax.ShapeDtypeStruct(fnet_out.shape, F32),
        input_output_aliases={5: 0},
        compiler_params=_cparams("parallel"),
        name="fnet_context",
    )(z, bf(cc), bf(sc), bf(cl), bf(sl), fnet_out)


def _s5_tables(lam_re, lam_im, log_step, b_re, b_im, c_re, c_im, d_skip):
    t = S5_T
    hp = lax.Precision.HIGHEST
    step = jnp.exp(log_step.astype(F32))[..., None]
    lr = lam_re.astype(F32) * step
    li = lam_im.astype(F32) * step
    k = jnp.arange(t + 1, dtype=F32)
    mag = jnp.exp(lr[..., None] * k)
    ang = li[..., None] * k
    pr, pi = mag * jnp.cos(ang), mag * jnp.sin(ang)
    ar, ai = pr[..., 1], pi[..., 1]
    den = lam_re.astype(F32) ** 2 + lam_im.astype(F32) ** 2
    qr = ((ar - 1.0) * lam_re + ai * lam_im) / den
    qi = (ai * lam_re - (ar - 1.0) * lam_im) / den
    br = qr[..., None] * b_re - qi[..., None] * b_im
    bi = qr[..., None] * b_im + qi[..., None] * b_re
    cr, ci = c_re.astype(F32), c_im.astype(F32)
    wr = pr[..., None] * br[..., None, :] - pi[..., None] * bi[..., None, :]
    wi = pr[..., None] * bi[..., None, :] + pi[..., None] * br[..., None, :]
    kern = (jnp.einsum('xgcn,xgnkd->xgkcd', cr, wr, precision=hp)
            - jnp.einsum('xgcn,xgnkd->xgkcd', ci, wi, precision=hp))
    tt = np.arange(t)
    lag_f = np.clip(tt[:, None] - tt[None, :], 0, None)
    lag_b = np.clip(tt[None, :] - tt[:, None], 0, None)
    kf = kern[0][:, lag_f]
    kb = kern[1][:, lag_b]
    diag = kern[0][:, 0] + kern[1][:, 0] + jax.vmap(jnp.diag)(d_skip.astype(F32))
    tri = (tt[:, None] - tt[None, :])[None, :, :, None, None]
    m = jnp.where(tri > 0, kf, jnp.where(tri < 0, kb, diag[:, None, None]))
    g = m.shape[0]
    mt = m.transpose(0, 2, 4, 1, 3).reshape(g, t * S5_C, t * S5_C)
    pf_r = wr[0][:, :, t - 1 - tt].transpose(0, 2, 3, 1).reshape(g, t * S5_C, S5_N)
    pf_i = wi[0][:, :, t - 1 - tt].transpose(0, 2, 3, 1).reshape(g, t * S5_C, S5_N)
    pb_r = wr[1][:, :, tt].transpose(0, 2, 3, 1).reshape(g, t * S5_C, S5_N)
    pb_i = wi[1][:, :, tt].transpose(0, 2, 3, 1).reshape(g, t * S5_C, S5_N)
    p = jnp.concatenate([pf_r, pf_i, pb_r, pb_i], axis=-1)
    def readout(x, idx):
        zr = cr[x][:, :, :, None] * pr[x][:, None, :, idx] - ci[x][:, :, :, None] * pi[x][:, None, :, idx]
        zi = cr[x][:, :, :, None] * pi[x][:, None, :, idx] + ci[x][:, :, :, None] * pr[x][:, None, :, idx]
        to_rows = lambda a: a.transpose(0, 2, 3, 1).reshape(g, S5_N, t * S5_C)
        return to_rows(zr), to_rows(-zi)
    qf_r, qf_i = readout(0, tt + 1)
    qb_r, qb_i = readout(1, t - tt)
    q = jnp.concatenate([qf_r, qf_i, qb_r, qb_i], axis=1)
    at_r, at_i = pr[..., t], pi[..., t]
    return mt.astype(BF16), p.astype(BF16), q.astype(BF16), at_r, at_i


def _s5_local_kernel(u_ref, p_ref, o_ref):
    o_ref[0] = jnp.dot(u_ref[0].astype(BF16), p_ref[0], preferred_element_type=F32)


def _s5_local(u, p):
    g, r, w = u.shape
    n4 = p.shape[2]
    return pl.pallas_call(
        _s5_local_kernel,
        grid=(g,),
        in_specs=[pl.BlockSpec((1, r, w), lambda i: (i, 0, 0)),
                  pl.BlockSpec((1, w, n4), lambda i: (i, 0, 0))],
        out_specs=pl.BlockSpec((1, r, n4), lambda i: (i, 0, 0)),
        out_shape=jax.ShapeDtypeStruct((g, r, n4), F32),
        compiler_params=_cparams("parallel"),
        name="s5_chunk_states",
    )(u, p)


def _s5_scan_kernel(sr_ref, si_ref, ar_ref, ai_ref, or_ref, oi_ref):
    ar, ai = ar_ref[...], ai_ref[...]

    def body(s, carry):
        hr, hi = carry
        or_ref[s] = hr
        oi_ref[s] = hi
        return (ar * hr - ai * hi + sr_ref[s], ar * hi + ai * hr + si_ref[s])

    zero = jnp.zeros(ar.shape, F32)
    lax.fori_loop(0, sr_ref.shape[0], body, (zero, zero))


def _s5_scan(sr, si, ar, ai):
    nch, rows, w = sr.shape
    wb = 512
    blk = pl.BlockSpec((nch, rows, wb), lambda i: (0, 0, i))
    cst = pl.BlockSpec((rows, wb), lambda i: (0, i))
    return pl.pallas_call(
        _s5_scan_kernel,
        grid=(w // wb,),
        in_specs=[blk, blk, cst, cst],
        out_specs=[blk, blk],
        out_shape=[jax.ShapeDtypeStruct(sr.shape, F32)] * 2,
        compiler_params=_cparams("parallel"),
        name="s5_chunk_scan",
    )(sr, si, ar, ai)


def _s5_apply_kernel(u_ref, mt_ref, h_ref, q_ref, o_ref):
    y = jnp.dot(u_ref[0].astype(BF16), mt_ref[0], preferred_element_type=F32)
    o_ref[0] = y + jnp.dot(h_ref[0].astype(BF16), q_ref[0], preferred_element_type=F32)


def _s5_apply(u, mt, hin, q):
    g, r, w = u.shape
    n4 = hin.shape[2]
    return pl.pallas_call(
        _s5_apply_kernel,
        grid=(g,),
        in_specs=[pl.BlockSpec((1, r, w), lambda i: (i, 0, 0)),
                  pl.BlockSpec((1, w, w), lambda i: (i, 0, 0)),
                  pl.BlockSpec((1, r, n4), lambda i: (i, 0, 0)),
                  pl.BlockSpec((1, n4, w), lambda i: (i, 0, 0))],
        out_specs=pl.BlockSpec((1, r, w), lambda i: (i, 0, 0)),
        out_shape=jax.ShapeDtypeStruct((g, r, w), F32),
        compiler_params=_cparams("parallel"),
        name="s5_apply",
    )(u, mt, hin, q)


def _glu_kernel(y_ref, w_ref, b_ref, o_ref):
    y = jax.nn.gelu(y_ref[...])
    gate = jnp.dot(y.astype(BF16), w_ref[...], preferred_element_type=F32) + b_ref[...]
    o_ref[...] = y * jax.nn.sigmoid(gate)


def _glu(y, w_bf16, b):
    nt, w = y.shape
    return pl.pallas_call(
        _glu_kernel,
        grid=(nt // TM,),
        in_specs=[pl.BlockSpec((TM, w), lambda i: (i, 0)),
                  pl.BlockSpec((w, w), lambda i: (0, 0)),
                  pl.BlockSpec((1, w), lambda i: (0, 0))],
        out_specs=pl.BlockSpec((TM, w), lambda i: (i, 0)),
        out_shape=jax.ShapeDtypeStruct((nt, w), F32),
        compiler_params=_cparams("parallel"),
        name="s5_glu",
    )(y, w_bf16, b.reshape(1, w))


def _s5_mix(z, nb, l, ctx, tables, glu_w, glu_b):
    mt, p, q, at_r, at_i = tables
    t = S5_T
    ncl, ncc = l // t, ctx // t
    nch = ncl + ncc
    u_lat = z[:nb * l, FNET_W:].reshape(nb, ncl, t, S5_G, S5_C)
    u_ctx = z[nb * l:, FNET_W:].reshape(nb, ncc, t, S5_G, S5_C)
    u = jnp.concatenate([u_ctx, u_lat], axis=1)
    u = u.transpose(3, 1, 0, 2, 4).reshape(S5_G, nch * nb, t * S5_C)
    s = _s5_local(u, p).reshape(S5_G, nch, nb, 4, S5_N)
    order_b = np.concatenate([np.arange(ncc - 1, -1, -1), np.arange(nch - 1, ncc - 1, -1)])
    inv_b = np.argsort(order_b)
    to_scan = lambda a: a.transpose(1, 2, 0, 3).reshape(nch, nb, S5_G * S5_N)
    pad = (-2 * nb) % SUBLANES
    def scan_in(j):
        a = jnp.concatenate([to_scan(s[:, :, :, j]), to_scan(s[:, order_b][:, :, :, 2 + j])], axis=1)
        return jnp.pad(a, ((0, 0), (0, pad), (0, 0)))
    def scan_const(a):
        rows = jnp.concatenate([jnp.broadcast_to(a[0].reshape(1, -1), (nb, S5_G * S5_N)),
                                jnp.broadcast_to(a[1].reshape(1, -1), (nb, S5_G * S5_N))], axis=0)
        return jnp.pad(rows, ((0, pad), (0, 0)))
    hr, hi = _s5_scan(scan_in(0), scan_in(1), scan_const(at_r), scan_const(at_i))
    from_scan = lambda a: a.reshape(nch, nb, S5_G, S5_N).transpose(2, 0, 1, 3)
    hin = jnp.stack([from_scan(hr[:, :nb]), from_scan(hi[:, :nb]),
                     from_scan(hr[:, nb:2 * nb])[:, inv_b], from_scan(hi[:, nb:2 * nb])[:, inv_b]], axis=3)
    hin = hin.reshape(S5_G, nch * nb, 4 * S5_N)
    y = _s5_apply(u, mt, hin, q).reshape(S5_G, nch, nb, t, S5_C).transpose(2, 1, 3, 0, 4)
    y_ctx = y[:, :ncc].reshape(nb * ctx, S5_W)
    y_lat = y[:, ncc:].reshape(nb * l, S5_W)
    return _glu(jnp.concatenate([y_lat, y_ctx], axis=0), glu_w.astype(BF16), glu_b)


def _sgu_kernel(z_ref, lng_ref, lnb_ref, ws_ref, bs_ref, o_ref):
    for ch in range(TM // SGU_CHUNK):
        rows = slice(ch * SGU_CHUNK, (ch + 1) * SGU_CHUNK)
        for g in range(SGU_GROUPS):
            cols = slice(g * LANES, (g + 1) * LANES)
            u = jax.nn.gelu(z_ref[rows, g * LANES:(g + 1) * LANES])
            v = jax.nn.gelu(z_ref[rows, SGU_W + g * LANES:SGU_W + (g + 1) * LANES])
            mu = jnp.mean(v, axis=-1, keepdims=True)
            vc = v - mu
            var = jnp.mean(vc * vc, axis=-1, keepdims=True)
            vn = vc * lax.rsqrt(var + EPS) * lng_ref[g:g + 1, :] + lnb_ref[g:g + 1, :]
            vm = jnp.dot(ws_ref[g], vn.astype(BF16), preferred_element_type=F32) + bs_ref[:, g:g + 1]
            o_ref[rows, cols] = u * vm


def _sgu(z, ln_g, ln_b, w_s, b_s):
    nt = z.shape[0]
    return pl.pallas_call(
        _sgu_kernel,
        grid=(nt // TM,),
        in_specs=[pl.BlockSpec((TM, 2 * SGU_W), lambda i: (i, 0)),
                  pl.BlockSpec((SGU_GROUPS, LANES), lambda i: (0, 0)),
                  pl.BlockSpec((SGU_GROUPS, LANES), lambda i: (0, 0)),
                  pl.BlockSpec((SGU_GROUPS, SGU_CHUNK, SGU_CHUNK), lambda i: (0, 0, 0)),
                  pl.BlockSpec((SGU_CHUNK, SGU_GROUPS), lambda i: (0, 0))],
        out_specs=pl.BlockSpec((TM, SGU_W), lambda i: (i, 0)),
        out_shape=jax.ShapeDtypeStruct((nt, SGU_W), F32),
        compiler_params=_cparams("parallel"),
        name="spatial_gating",
    )(z, ln_g, ln_b, w_s.astype(BF16), b_s.T)


def _rope_tables(l):
    t = np.arange(l)
    inv_freq = ROPE_BASE ** (-np.arange(0, ROPE_AXIS_DIM, 2, dtype=np.float32) / ROPE_AXIS_DIM)
    inv_freq = inv_freq.astype(np.float32)
    d = np.arange(LANES) % HEAD_DIM
    pos = np.where(d[None, :] < ROPE_AXIS_DIM, (t // GRID_W)[:, None], (t % GRID_W)[:, None]).astype(np.float32)
    ang = pos * inv_freq[d % (ROPE_AXIS_DIM // 2)][None, :]
    upper = (d % ROPE_AXIS_DIM) >= ROPE_AXIS_DIM // 2
    cos = np.cos(ang)
    sin_lo = np.where(upper[None, :], np.sin(ang), 0.0)
    sin_hi = np.where(upper[None, :], 0.0, -np.sin(ang))
    ident = lambda a, v: np.concatenate([a, np.full((TM, LANES), v)], axis=0).astype(np.float32)
    return jnp.asarray(ident(cos, 1.0)), jnp.asarray(ident(sin_lo, 0.0)), jnp.asarray(ident(sin_hi, 0.0))


def _qkv_kernel(q_ref, k_ref, v_ref, cos_ref, slo_ref, shi_ref, qo_ref, ko_ref, vo_ref):
    cos, slo, shi = cos_ref[...], slo_ref[...], shi_ref[...]
    half = ROPE_AXIS_DIM // 2
    for h in range(HEADS):
        cols = slice(h * LANES, (h + 1) * LANES)
        for src, dst, scale in ((q_ref, qo_ref, HEAD_DIM ** -0.5), (k_ref, ko_ref, 1.0)):
            x = src[:, cols]
            y = x * cos + pltpu.roll(x, half, 1) * slo + pltpu.roll(x, LANES - half, 1) * shi
            dst[:, cols] = (y * scale).astype(BF16)
    vo_ref[...] = v_ref[...].astype(BF16)


def _qkv(z, rope, geom):
    nt = z.shape[0]
    n_lat_tiles, tiles_per_batch, _ = geom
    tab = pl.BlockSpec((TM, LANES), lambda i: (jnp.where(i < n_lat_tiles, i % tiles_per_batch, tiles_per_batch), 0))
    col = lambda c: pl.BlockSpec((TM, QK_W), lambda i: (i, c))
    out = pl.BlockSpec((TM, QK_W), lambda i: (i, 0))
    return pl.pallas_call(
        _qkv_kernel,
        grid=(nt // TM,),
        in_specs=[col(Q_OFF // QK_W), col(KV_OFF // QK_W), col((KV_OFF + QK_W) // QK_W), tab, tab, tab],
        out_specs=[out, out, out],
        out_shape=[jax.ShapeDtypeStruct((nt, QK_W), BF16)] * 3,
        compiler_params=_cparams("parallel"),
        name="qkv_rope",
    )(z, z, z, *rope)


def _attn_kernel(lam_ref, q_ref, kc_ref, vc_ref, *rest, n_lat_chunks, out_scale):
    if n_lat_chunks:
        kl_ref, vl_ref, g_ref, o_ref = rest
    else:
        g_ref, _, o_ref = rest
    q = q_ref[...]
    lane = lax.broadcasted_iota(I32, q.shape, 1)
    zero = jnp.zeros_like(q)
    qs = jnp.concatenate([jnp.where(lane < HEAD_DIM, q, zero), jnp.where(lane >= HEAD_DIM, q, zero)], axis=0)
    nt_dims = (((1,), (1,)), ((), ()))

    def step(k, v, carry):
        m, l, a = carry
        s = lax.dot_general(qs, k, nt_dims, preferred_element_type=F32)
        mn = jnp.maximum(m, jnp.max(s, axis=-1, keepdims=True))
        alpha = jnp.exp(m - mn)
        p = jnp.exp(s - mn)
        l = alpha * l + jnp.sum(p, axis=-1, keepdims=True)
        a = alpha * a + jnp.dot(p.astype(BF16), v, preferred_element_type=F32)
        return mn, l, a

    rows = q.shape[0]
    init = (jnp.full((2 * rows, 1), NEG, F32), jnp.zeros((2 * rows, 1), F32), jnp.zeros((2 * rows, V_DIM), F32))
    carry = step(kc_ref[...], vc_ref[...], init)
    if n_lat_chunks:
        def body(c, carry):
            ks = pl.ds(pl.multiple_of(c * TK, TK), TK)
            return step(kl_ref[ks, :], vl_ref[ks, :], carry)
        carry = lax.fori_loop(0, n_lat_chunks, body, carry)
    _, l, a = carry
    o = a[:rows] / l[:rows] - lam_ref[0] * (a[rows:] / l[rows:])
    o_ref[...] = _rms(o, g_ref[...]) * out_scale


def _attention(q, k, v, lam, subln_g, nb, l, ctx, lam_init):
    nt = q.shape[0]
    g = subln_g.reshape(1, V_DIM)
    scale = 1.0 - lam_init
    smem = pl.BlockSpec(memory_space=pltpu.SMEM)
    off = nb * l // ctx
    ctx_blk = lambda f: pl.BlockSpec((ctx, LANES), f)
    gspec3 = pl.BlockSpec((1, V_DIM), lambda b, h, i: (0, 0))
    any_spec = pl.BlockSpec(memory_space=pl.ANY)
    nq = l // TQ
    lat = pl.pallas_call(
        functools.partial(_attn_kernel, n_lat_chunks=l // TK, out_scale=scale),
        grid=(nb, HEADS, nq),
        in_specs=[smem,
                  pl.BlockSpec((TQ, LANES), lambda b, h, i: (b * nq + i, h)),
                  ctx_blk(lambda b, h, i: (off + b, h)), ctx_blk(lambda b, h, i: (off + b, h)),
                  pl.BlockSpec((l, LANES), lambda b, h, i: (b, h)),
                  pl.BlockSpec((l, LANES), lambda b, h, i: (b, h)),
                  gspec3],
        out_specs=pl.BlockSpec((TQ, LANES), lambda b, h, i: (b * nq + i, h)),
        out_shape=jax.ShapeDtypeStruct((nt, HEADS * V_DIM), F32),
        compiler_params=_cparams("parallel", "parallel", "arbitrary"),
        name="diff_attention_latent",
    )
    ctx_call = pl.pallas_call(
        functools.partial(_attn_kernel, n_lat_chunks=0, out_scale=scale),
        grid=(nb, HEADS),
        in_specs=[smem,
                  ctx_blk(lambda b, h: (off + b, h)),
                  ctx_blk(lambda b, h: (off + b, h)), ctx_blk(lambda b, h: (off + b, h)),
                  pl.BlockSpec((1, V_DIM), lambda b, h: (0, 0)), any_spec],
        out_specs=ctx_blk(lambda b, h: (off + b, h)),
        out_shape=jax.ShapeDtypeStruct((nt, HEADS * V_DIM), F32),
        input_output_aliases={5: 0},
        compiler_params=_cparams("parallel", "parallel"),
        name="diff_attention_context",
    )
    return ctx_call(lam, q, k, v, g, lat(lam, q, k, v, k, v, g))


def _router_kernel(h_ref, g_ref, sh_ref, sc_ref, rw_ref, rb_ref, tri_ref, mrt_ref, meta_ref, cnt_ref, carry_ref):
    @pl.when(pl.program_id(0) == 0)
    def _():
        carry_ref[...] = jnp.zeros_like(carry_ref)

    m = _rms(h_ref[...], g_ref[...]) * (1.0 + sc_ref[...]) + sh_ref[...]
    for j in range(ROW_TILES):
        mrt_ref[pl.ds(j, TM, stride=ROW_TILES), :] = m[:, j * LANES:(j + 1) * LANES]
    logits = jnp.dot(m, rw_ref[...], preferred_element_type=F32, precision=lax.Precision.HIGHEST) + rb_ref[...]
    lane = lax.broadcasted_iota(I32, logits.shape, 1).astype(F32)
    multihot = jnp.zeros(logits.shape, F32)
    vals, idxs, sels = [], [], []
    for _ in range(TOP_K):
        mx = jnp.max(logits, axis=-1, keepdims=True)
        idx = jnp.min(jnp.where(logits == mx, lane, float(LANES)), axis=-1, keepdims=True)
        sel = lane == idx
        logits = jnp.where(sel, NEG, logits)
        multihot = jnp.where(sel, 1.0, multihot)
        vals.append(mx)
        idxs.append(idx)
        sels.append(sel)
    ex = [jnp.exp(v - vals[0]) for v in vals]
    den = ex[0] + ex[1] + ex[2] + ex[3]
    cum = jnp.dot(tri_ref[...], multihot.astype(BF16), preferred_element_type=F32) + carry_ref[0:1, :]
    meta = jnp.zeros(logits.shape, F32)
    for k in range(TOP_K):
        rank = jnp.sum(jnp.where(sels[k], cum, 0.0), axis=-1, keepdims=True)
        meta = jnp.where(lane == k, idxs[k], meta)
        meta = jnp.where(lane == TOP_K + k, rank, meta)
        meta = jnp.where(lane == 2 * TOP_K + k, ex[k] / den, meta)
    meta_ref[...] = meta
    carry_ref[...] = carry_ref[...] + jnp.sum(multihot, axis=0, keepdims=True)
    cnt_ref[...] = carry_ref[...]


def _router(h, g, shift, scale, router_w, router_b, geom):
    nt = h.shape[0]
    rw = jnp.pad(router_w.astype(F32), ((0, 0), (0, LANES - N_EXPERTS)))
    rb = jnp.pad(router_b.astype(F32), (0, LANES - N_EXPERTS), constant_values=-1.0e30).reshape(1, LANES)
    tri = jnp.asarray(np.tril(np.ones((TM, TM), np.float32), -1)).astype(BF16)
    return pl.pallas_call(
        _router_kernel,
        grid=(nt // TM,),
        in_specs=[pl.BlockSpec((TM, D), lambda i: (i, 0)),
                  pl.BlockSpec((1, D), lambda i: (0, 0)),
                  _mod_spec(geom), _mod_spec(geom),
                  pl.BlockSpec((D, LANES), lambda i: (0, 0)),
                  pl.BlockSpec((1, LANES), lambda i: (0, 0)),
                  pl.BlockSpec((TM, TM), lambda i: (0, 0))],
        out_specs=[pl.BlockSpec((TM * ROW_TILES, LANES), lambda i: (i, 0)),
                   pl.BlockSpec((TM, LANES), lambda i: (i, 0)),
                   pl.BlockSpec((SUBLANES, LANES), lambda i: (0, 0))],
        out_shape=[jax.ShapeDtypeStruct((nt * ROW_TILES, LANES), F32),
                   jax.ShapeDtypeStruct((nt, LANES), F32),
                   jax.ShapeDtypeStruct((SUBLANES, LANES), F32)],
        scratch_shapes=[pltpu.VMEM((SUBLANES, LANES), F32)],
        compiler_params=_cparams("arbitrary"),
        name="moe_router",
    )(h, g.reshape(1, D), shift, scale, rw, rb, tri)


def _dispatch_kernel(dest_ref, m_ref, zero_ref, x_ref, sem):
    del zero_ref

    def row_copy(r, dst_row):
        return pltpu.make_async_copy(
            m_ref.at[pl.ds(pl.multiple_of(r * ROW_TILES, ROW_TILES), ROW_TILES), :],
            x_ref.at[pl.ds(pl.multiple_of(dst_row * ROW_TILES, ROW_TILES), ROW_TILES), :], sem)

    def body(r, carry):
        for k in range(TOP_K):
            row_copy(r, dest_ref[0, r * TOP_K + k]).start()
        return carry

    lax.fori_loop(0, TM, body, 0)
    for _ in range(TOP_K):
        pltpu.make_async_copy(m_ref, x_ref.at[pl.ds(0, TM * ROW_TILES), :], sem).wait()


def _dispatch(dest, m_rt, n_rows):
    nt = m_rt.shape[0] // ROW_TILES
    n_tiles = nt // TM
    return pl.pallas_call(
        _dispatch_kernel,
        grid=(n_tiles,),
        in_specs=[pl.BlockSpec((None, 1, TM * TOP_K), lambda i: (i, 0, 0), memory_space=pltpu.SMEM),
                  pl.BlockSpec((TM * ROW_TILES, LANES), lambda i: (i, 0)),
                  pl.BlockSpec(memory_space=pl.ANY)],
        out_specs=pl.BlockSpec(memory_space=pl.ANY),
        out_shape=jax.ShapeDtypeStruct((n_rows * ROW_TILES, LANES), F32),
        scratch_shapes=[pltpu.SemaphoreType.DMA(())],
        input_output_aliases={2: 0},
        compiler_params=_cparams("arbitrary"),
        name="moe_dispatch",
    )(dest.reshape(n_tiles, 1, TM * TOP_K), m_rt, jnp.zeros((n_rows * ROW_TILES, LANES), F32))


def _expert_kernel(be_ref, nu_ref, x_ref, w1_ref, b1_ref, w2_ref, b2_ref, o_ref, w1b_ref, w2b_ref):
    b = pl.program_id(0)
    prev = be_ref[jnp.maximum(b - 1, 0)]
    fresh = jnp.logical_or(b == 0, be_ref[b] != prev)

    @pl.when(jnp.logical_and(fresh, b < nu_ref[0]))
    def _():
        w1b_ref[...] = w1_ref[0].astype(BF16)
        w2b_ref[...] = w2_ref[0].astype(BF16)

    @pl.when(b < nu_ref[0])
    def _():
        x = jnp.concatenate([x_ref[pl.ds(j, MOE_BLOCK, stride=ROW_TILES), :] for j in range(ROW_TILES)], axis=1)
        hh = jnp.dot(x.astype(BF16), w1b_ref[...], preferred_element_type=F32) + b1_ref[0]
        glu = jnp.minimum(hh[:, :D_EXPERT], SWIGLU_LIMIT)
        lin = jnp.clip(hh[:, D_EXPERT:], -SWIGLU_LIMIT, SWIGLU_LIMIT)
        act = glu * jax.nn.sigmoid(SWIGLU_ALPHA * glu) * (lin + 1.0)
        y = jnp.dot(act.astype(BF16), w2b_ref[...], preferred_element_type=F32) + b2_ref[0]
        for j in range(ROW_TILES):
            o_ref[pl.ds(j, MOE_BLOCK, stride=ROW_TILES), :] = y[:, j * LANES:(j + 1) * LANES]


def _experts(block_e, n_used, x_rt, w1, b1, w2, b2):
    n_blocks = block_e.shape[0]
    ne = w1.shape[0]
    blk = lambda b, be, nu: (jnp.minimum(b, nu[0] - 1), 0)
    exp3 = lambda b, be, nu: (be[jnp.minimum(b, nu[0] - 1)], 0, 0)
    grid_spec = pltpu.PrefetchScalarGridSpec(
        num_scalar_prefetch=2,
        grid=(n_blocks,),
        in_specs=[pl.BlockSpec((MOE_BLOCK * ROW_TILES, LANES), blk),
                  pl.BlockSpec((1, D, 2 * D_EXPERT), exp3),
                  pl.BlockSpec((1, 1, 2 * D_EXPERT), exp3),
                  pl.BlockSpec((1, D_EXPERT, D), exp3),
                  pl.BlockSpec((1, 1, D), exp3)],
        out_specs=pl.BlockSpec((MOE_BLOCK * ROW_TILES, LANES), blk),
        scratch_shapes=[pltpu.VMEM((D, 2 * D_EXPERT), BF16), pltpu.VMEM((D_EXPERT, D), BF16)],
    )
    return pl.pallas_call(
        _expert_kernel,
        grid_spec=grid_spec,
        out_shape=jax.ShapeDtypeStruct(x_rt.shape, F32),
        compiler_params=_cparams("arbitrary"),
        name="moe_experts",
    )(block_e, n_used, x_rt, w1, b1.reshape(ne, 1, -1), w2, b2.reshape(ne, 1, -1))


def _combine_kernel(dest_ref, y_ref, h_ref, meta_ref, gate_ref, fg_ref, o_ref, buf_ref, sem, *, final):
    def row_copy(k, r, src_row):
        return pltpu.make_async_copy(
            y_ref.at[pl.ds(pl.multiple_of(src_row * ROW_TILES, ROW_TILES), ROW_TILES), :],
            buf_ref.at[k, pl.ds(pl.multiple_of(r * ROW_TILES, ROW_TILES), ROW_TILES), :], sem.at[k])

    def body(r, carry):
        for k in range(TOP_K):
            row_copy(k, r, dest_ref[0, r * TOP_K + k]).start()
        return carry

    lax.fori_loop(0, TM, body, 0)
    for k in range(TOP_K):
        pltpu.make_async_copy(y_ref.at[pl.ds(0, TM * ROW_TILES), :], buf_ref.at[k], sem.at[k]).wait()
    meta = meta_ref[...]
    gates = [meta[:, 2 * TOP_K + k:2 * TOP_K + k + 1] for k in range(TOP_K)]
    outs = []
    for j in range(ROW_TILES):
        cols = slice(j * LANES, (j + 1) * LANES)
        f = gates[0] * buf_ref[0, pl.ds(j, TM, stride=ROW_TILES), :]
        for k in range(1, TOP_K):
            f = f + gates[k] * buf_ref[k, pl.ds(j, TM, stride=ROW_TILES), :]
        outs.append(h_ref[:, cols] + gate_ref[:, cols] * f)
    if final:
        hn = jnp.concatenate(outs, axis=1)
        o_ref[...] = _rms(hn, fg_ref[...])
    else:
        for j in range(ROW_TILES):
            o_ref[:, j * LANES:(j + 1) * LANES] = outs[j]


def _combine(dest, y_rt, h, meta, gate, final_g, geom, n_tiles, final):
    return pl.pallas_call(
        functools.partial(_combine_kernel, final=final),
        grid=(n_tiles,),
        in_specs=[pl.BlockSpec((None, 1, TM * TOP_K), lambda i: (i, 0, 0), memory_space=pltpu.SMEM),
                  pl.BlockSpec(memory_space=pl.ANY),
                  pl.BlockSpec((TM, D), lambda i: (i, 0)),
                  pl.BlockSpec((TM, LANES), lambda i: (i, 0)),
                  _mod_spec(geom),
                  pl.BlockSpec((1, D), lambda i: (0, 0))],
        out_specs=pl.BlockSpec((TM, D), lambda i: (i, 0)),
        out_shape=jax.ShapeDtypeStruct((n_tiles * TM, D), F32),
        scratch_shapes=[pltpu.VMEM((TOP_K, TM * ROW_TILES, LANES), F32), pltpu.SemaphoreType.DMA((TOP_K,))],
        compiler_params=_cparams("arbitrary"),
        name="moe_combine",
    )(dest.reshape(-1, 1, TM * TOP_K), y_rt, h, meta, gate, final_g.reshape(1, D))


def _moe(h, g, shift, scale, gate, router_w, router_b, w1, b1, w2, b2, final_g, geom, n_out_tiles, final):
    nt = h.shape[0]
    m_rt, meta, cnt = _router(h, g, shift, scale, router_w, router_b, geom)
    expert = meta[:, :TOP_K].astype(I32)
    rank = meta[:, TOP_K:2 * TOP_K].astype(I32)
    counts = cnt[0, :N_EXPERTS].astype(I32)
    padded = (counts + MOE_BLOCK - 1) // MOE_BLOCK * MOE_BLOCK
    pad_ends = jnp.cumsum(padded)
    pad_starts = pad_ends - padded
    dest = (pad_starts[expert] + rank).reshape(-1)
    n_blocks = nt * TOP_K // MOE_BLOCK + N_EXPERTS
    block_start = jnp.arange(n_blocks, dtype=I32) * MOE_BLOCK
    block_e = jnp.minimum(jnp.sum((pad_ends[None, :] <= block_start[:, None]).astype(I32), axis=1),
                          N_EXPERTS - 1).astype(I32)
    n_used = (pad_ends[-1:] // MOE_BLOCK).astype(I32)
    x_rt = _dispatch(dest, m_rt, n_blocks * MOE_BLOCK)
    y_rt = _experts(block_e, n_used, x_rt, w1, b1, w2, b2)
    return _combine(dest, y_rt, h, meta, gate, final_g, geom, n_out_tiles, final)


def kernel(x, c, ctx, c_ctx, ada_w, ada_b, norm1_g, norm2_g, ev_w_in, ev_w_out, s5_lam_re, s5_lam_im,
           s5_log_step, s5_b_re, s5_b_im, s5_c_re, s5_c_im, s5_d, s5_glu_w, s5_glu_b, od_w_in, od_w_out,
           sgu_ln_g, sgu_ln_b, sgu_w, sgu_b, diff_lq1, diff_lk1, diff_lq2, diff_lk2, diff_subln_g,
           router_w, router_b, exp_w1, exp_b1, exp_w2, exp_b2, final_g):
    nb, l, d = x.shape
    n_ctx = ctx.shape[1]
    depth = ada_w.shape[0]
    assert d == D and l % TM == 0 and n_ctx % TM == 0 and l % TK == 0
    assert TM % n_ctx == 0 or n_ctx % TM == 0
    n_lat = nb * l
    geom = (n_lat // TM, l // TM, nb)

    nbp = -(-(nb + 1) // SUBLANES) * SUBLANES
    cvec = jnp.zeros((nbp, D), F32).at[:nb].set(c).at[nb].set(c_ctx)
    mods = _adaln(cvec, ada_w, ada_b)
    mods = mods.reshape(depth, nbp, N_MOD, 1, D).transpose(0, 2, 1, 3, 4)

    h = jnp.concatenate([x.reshape(n_lat, D), ctx.reshape(nb * n_ctx, D)], axis=0)
    rope = _rope_tables(l)

    for i in range(depth):
        sh1, sc1, g1, sh2, sc2, g2 = (mods[i, j] for j in range(N_MOD))
        last = i == depth - 1
        j = i // 2
        if i % 2 == 0:
            z = _norm_in(h, norm1_g[i], sh1, sc1, ev_w_in[j].astype(BF16), geom)
            fnet = _fnet_ctx(z, _fnet_lat(z, nb, l), nb, l, n_ctx)
            tables = _s5_tables(s5_lam_re[j], s5_lam_im[j], s5_log_step[j], s5_b_re[j], s5_b_im[j],
                                s5_c_re[j], s5_c_im[j], s5_d[j])
            s5 = _s5_mix(z, nb, l, n_ctx, tables, s5_glu_w[j], s5_glu_b[j])
            h = _out_res(fnet, s5, ev_w_out[j].astype(BF16), h, g1, geom)
        else:
            lam_init = 0.8 - 0.6 * math.exp(-0.3 * i)
            z = _norm_in(h, norm1_g[i], sh1, sc1, od_w_in[j].astype(BF16), geom)
            sgu = _sgu(z, sgu_ln_g[j], sgu_ln_b[j], sgu_w[j], sgu_b[j])
            q, k, v = _qkv(z, rope, geom)
            lam = (jnp.exp(jnp.sum(diff_lq1[j].astype(F32) * diff_lk1[j].astype(F32)))
                   - jnp.exp(jnp.sum(diff_lq2[j].astype(F32) * diff_lk2[j].astype(F32))) + lam_init)
            attn = _attention(q, k, v, lam.reshape(1), diff_subln_g[j], nb, l, n_ctx, lam_init)
            h = _out_res(sgu, attn, od_w_out[j].astype(BF16), h, g1, geom)
        n_out_tiles = n_lat // TM if last else h.shape[0] // TM
        h = _moe(h, norm2_g[i], sh2, sc2, g2, router_w[i], router_b[i], exp_w1[i], exp_b1[i],
                 exp_w2[i], exp_b2[i], final_g, geom, n_out_tiles, last)
    return h.reshape(nb, l, D)
```
